```python
import math
import jax
import jax.numpy as jnp
from jax import lax
import numpy as np

D_MODEL = 1024
BATCH = 8
SEQ = 8192
DEPTH = 1
DEC_BATCH = 128
DEC_SEQ = 8
PAST_LEN = 8192
PAGE_SIZE = 128

N_META = 16
RET_HEADS = 4
RET_DK = 128
RET_DV = 256
RET_CHUNK = 128
ROPE_BASE = 10000.0
ATT_HEADS = 8
ATT_DH = 128
IDX_HEADS = 8
IDX_DIM = 64
TOPK_MAX = 256
Q_BLOCK = 128
REL_BUCKETS = 32
REL_MAX_DIST = 128
PEER_HEADS = 8
PEER_KEYS = 128
PEER_EXPERTS = PEER_KEYS * PEER_KEYS
PEER_DQ = 128
PEER_TOPK = 16
PEER_BLOCK = 256
EPS = 1e-6
NEG = -1e30

IN_SIZES = (RET_HEADS * RET_DK, RET_HEADS * RET_DK, RET_HEADS * RET_DV, RET_HEADS * RET_DV,
            ATT_HEADS * ATT_DH, ATT_HEADS * ATT_DH, ATT_HEADS * ATT_DH,
            IDX_HEADS * IDX_DIM, IDX_DIM, IDX_HEADS, D_MODEL, D_MODEL)
D_IN = sum(IN_SIZES)

kernel_name = 'hybrid_retention_dsa_peer_step'


def split_cols(z):
    out, o = [], 0
    for s in IN_SIZES:
        out.append(z[..., o:o + s])
        o += s
    return out


def rmsnorm(x, g):
    xf = x.astype(jnp.float32)
    y = xf * lax.rsqrt(jnp.mean(xf * xf, axis=-1, keepdims=True) + EPS)
    return (y * g.astype(jnp.float32)).astype(x.dtype)


def rope(x, pos):
    half = x.shape[-1] // 2
    inv = ROPE_BASE ** (-jnp.arange(half, dtype=jnp.float32) / half)
    ang = pos.astype(jnp.float32)[:, None] * inv[None, :]
    cos, sin = jnp.cos(ang)[:, None, :], jnp.sin(ang)[:, None, :]
    xf = x.astype(jnp.float32)
    x1, x2 = xf[..., :half], xf[..., half:]
    return jnp.concatenate([x1 * cos - x2 * sin, x1 * sin + x2 * cos], axis=-1)


def ret_log_decay():
    return jnp.log1p(-jnp.exp2(-5.0 - jnp.arange(RET_HEADS, dtype=jnp.float32)))


def retention_chunk(q, k, v, s0):
    C = q.shape[1]
    ld = ret_log_decay()
    i = jnp.arange(C, dtype=jnp.float32)
    diff = i[:, None] - i[None, :]
    dmat = jnp.where(diff >= 0, jnp.exp(ld[:, None, None] * jnp.maximum(diff, 0.0)[None]), 0.0)
    scores = jnp.einsum('bihd,bjhd->bhij', q, k) * dmat[None]
    intra = jnp.einsum('bhij,bjhe->bihe', scores, v)
    decay_q = jnp.exp((i[:, None] + 1.0) * ld[None, :])
    inter = jnp.einsum('bihd,bhde->bihe', q, s0) * decay_q[None, :, :, None]
    decay_k = jnp.exp((C - 1.0 - i)[:, None] * ld[None, :])
    s1 = s0 * jnp.exp(C * ld)[None, :, None, None] + jnp.einsum('bjhd,bjhe->bhde', k * decay_k[None, :, :, None], v)
    return intra + inter, s1


def retention_prompt(q, k, v):
    B = q.shape[0]
    s0 = jnp.zeros((B, RET_HEADS, RET_DK, RET_DV), jnp.float32)
    o_meta, s = retention_chunk(q[:, :N_META], k[:, :N_META], v[:, :N_META], s0)

    def to_chunks(a):
        a = a[:, N_META:]
        return a.reshape((B, -1, RET_CHUNK) + a.shape[2:]).swapaxes(0, 1)

    def body(state, qkv):
        o, state = retention_chunk(qkv[0], qkv[1], qkv[2], state)
        return state, o

    s, o = lax.scan(body, s, (to_chunks(q), to_chunks(k), to_chunks(v)))
    o = o.swapaxes(0, 1).reshape(B, -1, RET_HEADS, RET_DV)
    return jnp.concatenate([o_meta, o], axis=1), s


def t5_bucket(dist):
    max_exact = REL_BUCKETS // 2
    d = jnp.maximum(dist, 0)
    df = jnp.maximum(d, 1).astype(jnp.float32)
    large = max_exact + (jnp.log(df / max_exact) / math.log(REL_MAX_DIST / max_exact)
                         * (REL_BUCKETS - max_exact)).astype(jnp.int32)
    large = jnp.minimum(large, REL_BUCKETS - 1)
    return jnp.where(d < max_exact, d, large)


def rel_bias_logits(dist, rel_bias):
    return jnp.moveaxis(rel_bias[t5_bucket(dist)], -1, -2).astype(jnp.float32)


def indexer_scores(iq, iw, ik):
    s = jax.nn.relu(jnp.einsum('bqhd,bld->bqhl', iq, ik).astype(jnp.float32))
    return jnp.einsum('bqhl,bqh->bql', s, iw.astype(jnp.float32))


def select_keys(scores, q_pos, n_top):
    k_pos = jnp.arange(scores.shape[-1], dtype=jnp.int32)
    visible = k_pos[None, :] <= q_pos[:, None]
    _, sel = lax.top_k(jnp.where(visible, scores, -jnp.inf), n_top)
    return sel, sel <= q_pos[:, None]


def dsa_prompt(q, k, v, iq, iw, ik, rel_bias, n_top):
    B, T = q.shape[:2]
    nblk = -(-T // Q_BLOCK)
    padq = nblk * Q_BLOCK - T
    pad = lambda a: jnp.pad(a, [(0, 0), (0, padq)] + [(0, 0)] * (a.ndim - 2))
    qp, iqp, iwp = pad(q), pad(iq), pad(iw)
    scale = ATT_DH ** -0.5

    def block(idx):
        b = idx // nblk
        start = (idx % nblk) * Q_BLOCK
        take_q = lambda a: lax.dynamic_slice(a, (b, start) + (0,) * (a.ndim - 2), (1, Q_BLOCK) + a.shape[2:])
        take_b = lambda a: lax.dynamic_slice_in_dim(a, b, 1, axis=0)
        q_pos = start + jnp.arange(Q_BLOCK, dtype=jnp.int32)
        kb, vb = take_b(k)[0], take_b(v)[0]
        sel, valid = select_keys(indexer_scores(take_q(iqp), take_q(iwp), take_b(ik)), q_pos, n_top)
        logits = (jnp.einsum('bqhd,bqkhd->bqhk', take_q(qp), kb[sel]).astype(jnp.float32) * scale
                  + rel_bias_logits(q_pos[:, None] - sel, rel_bias))
        p = jax.nn.softmax(jnp.where(valid[:, :, None, :], logits, NEG), axis=-1)
        return jnp.einsum('bqhk,bqkhd->bqhd', p.astype(v.dtype), vb[sel])[0]

    out = lax.map(block, jnp.arange(B * nblk, dtype=jnp.int32))
    return out.reshape(B, nblk * Q_BLOCK, ATT_HEADS, ATT_DH)[:, :T]


def dsa_sample(q, k_new, v_new, iq, iw, ik_new, cache_k, cache_v, cache_kidx, page_table, layer, rel_bias):
    B, S = q.shape[:2]
    past = page_table.shape[1] * PAGE_SIZE
    n_top = min(TOPK_MAX, (past + S) // 4)
    scale = ATT_DH ** -0.5
    ik_past = cache_kidx[layer, page_table].reshape(B, past, IDX_DIM)
    ik_all = jnp.concatenate([ik_past.astype(ik_new.dtype), ik_new], axis=1)
    q_pos = past + jnp.arange(S, dtype=jnp.int32)
    sel, valid = select_keys(indexer_scores(iq, iw, ik_all), q_pos, n_top)
    from_past = valid & (sel < past)
    sp = jnp.minimum(sel, past - 1)
    bi = jnp.arange(B)[:, None, None]
    phys = page_table[bi, sp // PAGE_SIZE]
    off = sp % PAGE_SIZE
    ks = cache_k[layer, phys, off].astype(q.dtype)
    vs = cache_v[layer, phys, off].astype(v_new.dtype)
    lp = (jnp.einsum('bqhd,bqkhd->bqhk', q, ks).astype(jnp.float32) * scale
          + rel_bias_logits(q_pos[:, None] - sel, rel_bias))
    lp = jnp.where(from_past[:, :, None, :], lp, NEG)
    chosen_new = jnp.any((sel[..., None] == q_pos) & valid[..., None], axis=2)
    ln = (jnp.einsum('bqhd,bjhd->bqhj', q, k_new).astype(jnp.float32) * scale
          + rel_bias_logits(q_pos[:, None] - q_pos[None, :], rel_bias))
    ln = jnp.where(chosen_new[:, :, None, :], ln, NEG)
    p = jax.nn.softmax(jnp.concatenate([lp, ln], axis=-1), axis=-1).astype(v_new.dtype)
    return (jnp.einsum('bqhk,bqkhd->bqhd', p[..., :n_top], vs)
            + jnp.einsum('bqhj,bjhd->bqhd', p[..., n_top:], v_new))


def project_in(h, w_in, q_norm_g, k_norm_g, pos):
    B, T, _ = h.shape
    rq, rk, rv, rg, aq, ak, av, iq, ik, iw, ga, gb = split_cols(h @ w_in)
    heads = lambda a, n: a.reshape(B, T, n, -1)
    r_q = rope(heads(rq, RET_HEADS), pos)
    r_k = rope(heads(rk, RET_HEADS), pos) * RET_DK ** -0.5
    r_v = heads(rv, RET_HEADS).astype(jnp.float32)
    a_q = rmsnorm(heads(aq, ATT_HEADS), q_norm_g)
    a_k = rmsnorm(heads(ak, ATT_HEADS), k_norm_g)
    a_v = heads(av, ATT_HEADS)
    i_q = heads(iq, IDX_HEADS) * IDX_DIM ** -0.5
    i_w = iw * IDX_HEADS ** -0.5
    return (r_q, r_k, r_v, rg, a_q, a_k, a_v, i_q, ik, i_w, ga, gb)


def merge_out(ret_o, att_o, rg, ga, gb, ret_norm_g, w_ret_o, w_att_o, w_out):
    B, T = rg.shape[:2]
    dt = rg.dtype
    y_r = rmsnorm(ret_o, ret_norm_g).reshape(B, T, RET_HEADS * RET_DV)
    y_r = (jax.nn.silu(rg.astype(jnp.float32)) * y_r).astype(dt) @ w_ret_o
    y_a = att_o.reshape(B, T, ATT_HEADS * ATT_DH) @ w_att_o
    merged = jax.nn.sigmoid(ga) * y_r + jax.nn.sigmoid(gb) * y_a
    return (merged @ w_out).astype(dt)


def peer(x, w_q, sub_keys, u_tab, v_tab):
    shp = x.shape
    xf = x.reshape(-1, D_MODEL)
    n = xf.shape[0]
    nb = -(-n // PEER_BLOCK)
    xb_all = jnp.pad(xf, ((0, nb * PEER_BLOCK - n), (0, 0))).reshape(nb, PEER_BLOCK, D_MODEL)
    half = PEER_DQ // 2
    sk1 = sub_keys[:, 0].astype(jnp.float32)
    sk2 = sub_keys[:, 1].astype(jnp.float32)

    def block(xb):
        qh = (xb @ w_q).reshape(PEER_BLOCK, PEER_HEADS, PEER_DQ).astype(jnp.float32)
        v1, i1 = lax.top_k(jnp.einsum('nhd,hkd->nhk', qh[..., :half], sk1), PEER_TOPK)
        v2, i2 = lax.top_k(jnp.einsum('nhd,hkd->nhk', qh[..., half:], sk2), PEER_TOPK)
        cand = (v1[..., :, None] + v2[..., None, :]).reshape(PEER_BLOCK, PEER_HEADS, PEER_TOPK * PEER_TOPK)
        sc, ci = lax.top_k(cand, PEER_TOPK)
        e = (jnp.take_along_axis(i1, ci // PEER_TOPK, axis=-1) * PEER_KEYS
             + jnp.take_along_axis(i2, ci % PEER_TOPK, axis=-1))
        g = jax.nn.softmax(sc, axis=-1)
        act = jax.nn.gelu(jnp.einsum('nd,nhkd->nhk', xb, u_tab[e]).astype(jnp.float32), approximate=False)
        return jnp.einsum('nhk,nhkd->nd', (g * act).astype(xb.dtype), v_tab[e])

    return lax.map(block, xb_all).reshape(-1, D_MODEL)[:n].reshape(shp).astype(x.dtype)


def setup_inputs(seed: int = 0) -> dict:
    key = jax.random.key(seed)
    ks = jax.random.split(key, 24)
    f32 = jnp.float32
    n_pages = PAST_LEN // PAGE_SIZE
    n_used = DEC_BATCH * n_pages
    n_pool = n_used + (n_used + 3) // 4
    nrm = lambda k, shape, s: jax.random.normal(k, shape, f32) * s
    gain = lambda k, shape: 1.0 + 0.05 * jax.random.normal(k, shape, f32)
    page_table = jax.random.permutation(ks[6], n_pool)[:n_used].reshape(DEC_BATCH, n_pages).astype(jnp.int32)
    return {
        'x_prompt': nrm(ks[0], (BATCH, SEQ, D_MODEL), 1.0),
        'x_sample': nrm(ks[1], (DEC_BATCH, DEC_SEQ, D_MODEL), 1.0),
        'cache_k': nrm(ks[2], (DEPTH, n_pool, PAGE_SIZE, ATT_HEADS, ATT_DH), 1.0),
        'cache_v': nrm(ks[3], (DEPTH, n_pool, PAGE_SIZE, ATT_HEADS, ATT_DH), 1.0),
        'cache_kidx': nrm(ks[4], (DEPTH, n_pool, PAGE_SIZE, IDX_DIM), 1.0),
        'state_ret': nrm(ks[5], (DEPTH, DEC_BATCH, RET_HEADS, RET_DK, RET_DV), 0.5),
        'page_table': page_table,
        'meta_tokens': nrm(ks[7], (N_META, D_MODEL), 1.0),
        'rel_bias': nrm(ks[8], (REL_BUCKETS, ATT_HEADS), 0.5),
        'attn_norm_g': gain(ks[9], (DEPTH, D_MODEL)),
        'w_in': nrm(ks[10], (DEPTH, D_MODEL, D_IN), D_MODEL ** -0.5),
        'q_norm_g': gain(ks[11], (DEPTH, ATT_DH)),
        'k_norm_g': gain(ks[12], (DEPTH, ATT_DH)),
        'ret_norm_g': gain(ks[13], (DEPTH, RET_HEADS, RET_DV)),
        'w_ret_o': nrm(ks[14], (DEPTH, RET_HEADS * RET_DV, D_MODEL), (RET_HEADS * RET_DV) ** -0.5),
        'w_att_o': nrm(ks[15], (DEPTH, ATT_HEADS * ATT_DH, D_MODEL), (ATT_HEADS * ATT_DH) ** -0.5),
        'w_out': nrm(ks[16], (DEPTH, D_MODEL, D_MODEL), D_MODEL ** -0.5),
        'ffn_norm_g': gain(ks[17], (DEPTH, D_MODEL)),
        'peer_w_q': nrm(ks[18], (DEPTH, D_MODEL, PEER_HEADS * PEER_DQ), D_MODEL ** -0.5),
        'peer_sub_keys': nrm(ks[19], (DEPTH, PEER_HEADS, 2, PEER_KEYS, PEER_DQ // 2), (PEER_DQ // 2) ** -0.5),
        'peer_u': nrm(ks[20], (DEPTH, PEER_EXPERTS, D_MODEL), D_MODEL ** -0.5),
        'peer_v': nrm(ks[21], (DEPTH, PEER_EXPERTS, D_MODEL), PEER_HEADS ** -0.5),
    }


def reference(x_prompt, x_sample, cache_k, cache_v, cache_kidx, state_ret, page_table,
              meta_tokens, rel_bias, attn_norm_g, w_in, q_norm_g, k_norm_g, ret_norm_g,
              w_ret_o, w_att_o, w_out, ffn_norm_g, peer_w_q, peer_sub_keys, peer_u, peer_v):
    B, S_p, _ = x_prompt.shape
    S_d = x_sample.shape[1]
    past = page_table.shape[1] * PAGE_SIZE
    n_top_p = min(TOPK_MAX, S_p // 4)
    meta = jnp.broadcast_to(meta_tokens.astype(x_prompt.dtype)[None], (B, N_META, D_MODEL))
    xp = jnp.concatenate([meta, x_prompt], axis=1)
    pos_p = jnp.arange(S_p + N_META, dtype=jnp.int32)
    pos_d = past + jnp.arange(S_d, dtype=jnp.int32)
    xd = x_sample
    k_p, v_p, ik_p, s_p, k_d, v_d, ik_d, s_d = [], [], [], [], [], [], [], []
    for l in range(DEPTH):
        h = rmsnorm(xp, attn_norm_g[l])
        rq, rk, rv, rg, aq, ak, av, iq, ik, iw, ga, gb = project_in(h, w_in[l], q_norm_g[l], k_norm_g[l], pos_p)
        ret_o, s_fin = retention_prompt(rq, rk, rv)
        att_o = dsa_prompt(aq, ak, av, iq, iw, ik, rel_bias, n_top_p)
        xp = xp + merge_out(ret_o, att_o, rg, ga, gb, ret_norm_g[l], w_ret_o[l], w_att_o[l], w_out[l])
        xp = xp + peer(rmsnorm(xp, ffn_norm_g[l]), peer_w_q[l], peer_sub_keys[l], peer_u[l], peer_v[l])
        k_p.append(ak.astype(cache_k.dtype))
        v_p.append(av.astype(cache_v.dtype))
        ik_p.append(ik.astype(cache_kidx.dtype))
        s_p.append(s_fin.astype(state_ret.dtype))
        h = rmsnorm(xd, attn_norm_g[l])
        rq, rk, rv, rg, aq, ak, av, iq, ik, iw, ga, gb = project_in(h, w_in[l], q_norm_g[l], k_norm_g[l], pos_d)
        ret_o, s_new = retention_chunk(rq, rk, rv, state_ret[l].astype(jnp.float32))
        att_o = dsa_sample(aq, ak, av, iq, iw, ik, cache_k, cache_v, cache_kidx, page_table, l, rel_bias)
        xd = xd + merge_out(ret_o, att_o, rg, ga, gb, ret_norm_g[l], w_ret_o[l], w_att_o[l], w_out[l])
        xd = xd + peer(rmsnorm(xd, ffn_norm_g[l]), peer_w_q[l], peer_sub_keys[l], peer_u[l], peer_v[l])
        k_d.append(ak.astype(cache_k.dtype))
        v_d.append(av.astype(cache_v.dtype))
        ik_d.append(ik.astype(cache_kidx.dtype))
        s_d.append(s_new.astype(state_ret.dtype))
    y_prompt = xp[:, N_META:]
    y_sample = xd
    return (y_prompt, y_sample, jnp.stack(k_p), jnp.stack(v_p), jnp.stack(ik_p), jnp.stack(s_p),
            jnp.stack(k_d), jnp.stack(v_d), jnp.stack(ik_d), jnp.stack(s_d))
```

```python
import functools
import math

import jax
import jax.numpy as jnp
from jax import lax
from jax.experimental import pallas as pl
from jax.experimental.pallas import tpu as pltpu

F32 = jnp.float32
BF = jnp.bfloat16
I32 = jnp.int32

D_MODEL = 1024
PAGE_SIZE = 128
N_META = 16
RET_HEADS = 4
RET_DK = 128
RET_DV = 256
RET_CHUNK = 128
ROPE_BASE = 10000.0
ATT_HEADS = 8
ATT_DH = 128
IDX_HEADS = 8
IDX_DIM = 64
TOPK_MAX = 256
REL_BUCKETS = 32
REL_MAX_DIST = 128
PEER_HEADS = 8
PEER_KEYS = 128
PEER_DQ = 128
PEER_TOPK = 16
EPS = 1e-6
NEG = -1e30

LANE = 128
TOK_TILE = 512
QT = 128
KBLK = 512
EXP_CHUNK = 1024
PG_THR = 8
PG_ATT = 4
NEW_PAD = 16
VMEM_LIMIT = 52 * 1024 * 1024
INT_MIN = -2 ** 31
KEY_NEG_INF = -2139095041

_NT = (((1,), (1,)), ((), ()))
_TN = (((0,), (0,)), ((), ()))


def _cparams(sem):
    return pltpu.CompilerParams(dimension_semantics=sem, vmem_limit_bytes=VMEM_LIMIT)


def _rms(x, g):
    return x * lax.rsqrt(jnp.mean(x * x, axis=-1, keepdims=True) + EPS) * g


def _sort_key(s):
    b = lax.bitcast_convert_type(s, I32)
    return b ^ ((b >> 31) & 0x7FFFFFFF)


def _key_to_float(k):
    return lax.bitcast_convert_type(k ^ ((k >> 31) & 0x7FFFFFFF), F32)


def _proj_ret_kernel(x_ref, g_ref, w_ref, cos_ref, sin_ref, rq_ref, rk_ref, rv_ref, rg_ref):
    h = _rms(x_ref[...], g_ref[...]).astype(BF)
    cosf = cos_ref[...]
    sinf = sin_ref[...]
    nqk = RET_HEADS * RET_DK
    for off, ref, sc in ((0, rq_ref, None), (nqk, rk_ref, RET_DK ** -0.5)):
        z = jnp.dot(h, w_ref[:, off:off + nqk], preferred_element_type=F32)
        for hd in range(RET_HEADS):
            sl = slice(hd * RET_DK, (hd + 1) * RET_DK)
            xh = z[:, sl]
            r = xh * cosf + pltpu.roll(xh, RET_DK // 2, 1) * sinf
            ref[:, sl] = r if sc is None else r * sc
    nv = RET_HEADS * RET_DV
    rv_ref[...] = jnp.dot(h, w_ref[:, 2 * nqk:2 * nqk + nv], preferred_element_type=F32)
    rg_ref[...] = jnp.dot(h, w_ref[:, 2 * nqk + nv:2 * nqk + 2 * nv], preferred_element_type=F32)


def _proj_att_kernel(x_ref, g_ref, w_ref, qg_ref, kg_ref, aq_ref, ak_ref, av_ref, akb_ref, avb_ref):
    h = _rms(x_ref[...], g_ref[...]).astype(BF)
    n = ATT_HEADS * ATT_DH
    q = jnp.dot(h, w_ref[:, 0:n], preferred_element_type=F32)
    k = jnp.dot(h, w_ref[:, n:2 * n], preferred_element_type=F32)
    for hd in range(ATT_HEADS):
        sl = slice(hd * ATT_DH, (hd + 1) * ATT_DH)
        aq_ref[:, sl] = _rms(q[:, sl], qg_ref[...]).astype(BF)
        kn = _rms(k[:, sl], kg_ref[...])
        ak_ref[:, sl] = kn
        akb_ref[:, sl] = kn.astype(BF)
    v = jnp.dot(h, w_ref[:, 2 * n:3 * n], preferred_element_type=F32)
    av_ref[...] = v
    avb_ref[...] = v.astype(BF)


def _proj_idx_kernel(x_ref, g_ref, w_ref, iq_ref, ik_ref, ikb_ref, iw_ref, ga_ref, gb_ref):
    h = _rms(x_ref[...], g_ref[...]).astype(BF)
    nq = IDX_HEADS * LANE
    iq = jnp.dot(h, w_ref[:, 0:nq], preferred_element_type=F32)
    iq_ref[...] = (iq * IDX_DIM ** -0.5).astype(BF)
    ik = jnp.dot(h, w_ref[:, nq:nq + LANE], preferred_element_type=F32)
    ik_ref[...] = ik[:, 0:IDX_DIM]
    ikb_ref[...] = ik.astype(BF)
    iw = jnp.dot(h, w_ref[:, nq + LANE:nq + 2 * LANE], preferred_element_type=F32)
    iw_ref[...] = iw * IDX_HEADS ** -0.5
    o = nq + 2 * LANE
    ga_ref[...] = jnp.dot(h, w_ref[:, o:o + D_MODEL], preferred_element_type=F32)
    gb_ref[...] = jnp.dot(h, w_ref[:, o + D_MODEL:o + 2 * D_MODEL], preferred_element_type=F32)


def _in_proj(x2d, norm_g, wts, cosf, sinf, qg, kg):
    n = x2d.shape[0]
    t = min(TOK_TILE, n)
    assert n % t == 0 and cosf.shape[0] % t == 0
    nt = n // t
    ptiles = cosf.shape[0] // t
    w1, w2, w3 = wts
    row = lambda width: pl.BlockSpec((t, width), lambda i: (i, 0))
    full = lambda a: pl.BlockSpec(a.shape, lambda i: (0,) * a.ndim)
    tab = pl.BlockSpec((t, LANE), lambda i: (i % ptiles, 0))
    sds = lambda width, dt: jax.ShapeDtypeStruct((n, width), dt)
    g = norm_g.reshape(1, D_MODEL)
    cp = _cparams(("parallel",))

    rq, rk, rv, rg = pl.pallas_call(
        _proj_ret_kernel, grid=(nt,),
        in_specs=[row(D_MODEL), full(g), full(w1), tab, tab],
        out_specs=[row(512), row(512), row(1024), row(1024)],
        out_shape=[sds(512, F32), sds(512, F32), sds(1024, F32), sds(1024, F32)],
        compiler_params=cp, name="proj_ret")(x2d, g, w1, cosf, sinf)

    aq, ak, av, akb, avb = pl.pallas_call(
        _proj_att_kernel, grid=(nt,),
        in_specs=[row(D_MODEL), full(g), full(w2), full(qg), full(kg)],
        out_specs=[row(1024)] * 5,
        out_shape=[sds(1024, BF), sds(1024, F32), sds(1024, F32), sds(1024, BF), sds(1024, BF)],
        compiler_params=cp, name="proj_att")(x2d, g, w2, qg, kg)

    iq, ik, ikb, iw, ga, gb = pl.pallas_call(
        _proj_idx_kernel, grid=(nt,),
        in_specs=[row(D_MODEL), full(g), full(w3)],
        out_specs=[row(1024), row(IDX_DIM), row(LANE), row(LANE), row(1024), row(1024)],
        out_shape=[sds(1024, BF), sds(IDX_DIM, F32), sds(LANE, BF), sds(LANE, F32),
                   sds(1024, F32), sds(1024, F32)],
        compiler_params=cp, name="proj_idx")(x2d, g, w3)

    return dict(rq=rq, rk=rk, rv=rv, rg=rg, aq=aq, ak=ak, av=av, akb=akb, avb=avb,
                iq=iq, ik=ik, ikb=ikb, iw=iw, ga=ga, gb=gb)


def _ret_kernel(q_ref, k_ref, v_ref, s0_ref, dmat_ref, dq_ref, dk_ref, gc_ref, o_ref, sout_ref, st_ref):
    c = pl.program_id(1)

    @pl.when(c == 0)
    def _():
        st_ref[...] = s0_ref[0]

    for h in range(RET_HEADS):
        qh = q_ref[0, :, h * RET_DK:(h + 1) * RET_DK].astype(BF)
        kh = k_ref[0, :, h * RET_DK:(h + 1) * RET_DK]
        vh = v_ref[0, :, h * RET_DV:(h + 1) * RET_DV].astype(BF)
        s_old = st_ref[h]
        sc = lax.dot_general(qh, kh.astype(BF), _NT, preferred_element_type=F32) * dmat_ref[h]
        intra = jnp.dot(sc.astype(BF), vh, preferred_element_type=F32)
        inter = jnp.dot(qh, s_old.astype(BF), preferred_element_type=F32) * dq_ref[h]
        o_ref[0, :, h * RET_DV:(h + 1) * RET_DV] = intra + inter
        kd = (kh * dk_ref[h]).astype(BF)
        upd = lax.dot_general(kd, vh, _TN, preferred_element_type=F32)
        st_ref[h] = s_old * gc_ref[h] + upd

    @pl.when(c == pl.num_programs(1) - 1)
    def _():
        sout_ref[0] = st_ref[...]


def _retention(q, k, v, s0, chunk):
    b, t, _ = q.shape
    assert t % chunk == 0
    nc = t // chunk
    ld = jnp.log1p(-jnp.exp2(-5.0 - jnp.arange(RET_HEADS, dtype=F32)))
    i = jnp.arange(chunk, dtype=F32)
    diff = i[:, None] - i[None, :]
    dmat = jnp.where(diff >= 0, jnp.exp(ld[:, None, None] * jnp.maximum(diff, 0.0)[None]), 0.0)
    dq = jnp.exp((i[None, :] + 1.0) * ld[:, None])[..., None]
    dk = jnp.exp((chunk - 1.0 - i)[None, :] * ld[:, None])[..., None]
    gc = jnp.exp(chunk * ld).reshape(RET_HEADS, 1, 1)
    shared = s0.shape[0] == 1
    full = lambda a: pl.BlockSpec(a.shape, lambda bi, ci: (0,) * a.ndim)
    seq = lambda width: pl.BlockSpec((1, chunk, width), lambda bi, ci: (bi, ci, 0))
    st_spec = pl.BlockSpec((1, RET_HEADS, RET_DK, RET_DV),
                           (lambda bi, ci: (0, 0, 0, 0)) if shared else (lambda bi, ci: (bi, 0, 0, 0)))
    return pl.pallas_call(
        _ret_kernel, grid=(b, nc),
        in_specs=[seq(RET_HEADS * RET_DK), seq(RET_HEADS * RET_DK), seq(RET_HEADS * RET_DV),
                  st_spec, full(dmat), full(dq), full(dk), full(gc)],
        out_specs=[seq(RET_HEADS * RET_DV),
                   pl.BlockSpec((1, RET_HEADS, RET_DK, RET_DV), lambda bi, ci: (bi, 0, 0, 0))],
        out_shape=[jax.ShapeDtypeStruct((b, t, RET_HEADS * RET_DV), F32),
                   jax.ShapeDtypeStruct((b, RET_HEADS, RET_DK, RET_DV), F32)],
        scratch_shapes=[pltpu.VMEM((RET_HEADS, RET_DK, RET_DV), F32)],
        compiler_params=_cparams(("parallel", "arbitrary")), name="retention",
    )(q, k, v, s0, dmat, dq, dk, gc)


def _t5_bias_by_dist(rel_bias, maxd):
    d = jnp.arange(maxd + 1, dtype=I32)
    max_exact = REL_BUCKETS // 2
    df = jnp.maximum(d, 1).astype(F32)
    large = max_exact + (jnp.log(df / max_exact) / math.log(REL_MAX_DIST / max_exact)
                         * (REL_BUCKETS - max_exact)).astype(I32)
    large = jnp.minimum(large, REL_BUCKETS - 1)
    bucket = jnp.where(d < max_exact, d, large)
    return rel_bias[bucket].T.astype(F32)


def _idx_scores(iq, wb_ref, ikt):
    s = None
    for h in range(IDX_HEADS):
        r = lax.dot_general(iq[:, h * LANE:(h + 1) * LANE], ikt, _NT, preferred_element_type=F32)
        r = jnp.maximum(r, 0.0) * wb_ref[h]
        s = r if s is None else s + r
    return s + 0.0


def _fill_wb(wb_ref, iw):
    for h in range(IDX_HEADS):
        wb_ref[h] = jnp.broadcast_to(iw[:, h:h + 1], (iw.shape[0], LANE))


def _kth_largest_key(keys_ref, ntiles, n_top, rows):
    def step(sidx, t):
        bitv = jnp.left_shift(jnp.int32(1), 31 - sidx)
        cand = ((t ^ INT_MIN) | bitv) ^ INT_MIN
        cb = jnp.broadcast_to(cand, (rows, LANE))

        def count(j, acc):
            return acc + jnp.where(keys_ref[j] >= cb, 1.0, 0.0)

        acc = lax.fori_loop(0, ntiles, count, jnp.zeros((rows, LANE), F32))
        cnt = jnp.sum(acc, axis=1, keepdims=True)
        return jnp.where(cnt >= n_top, cand, t)

    t = lax.fori_loop(0, 32, step, jnp.full((rows, 1), INT_MIN, I32))
    return jnp.maximum(t, KEY_NEG_INF)


def _thr_prompt_kernel(iq_ref, iw_ref, ik_ref, ikm_ref, thr_ref, keys_ref, wb_ref, *, n_top):
    i = pl.program_id(1)
    iq = iq_ref[0]
    _fill_wb(wb_ref, iw_ref[0])
    col = lax.broadcasted_iota(I32, (QT, LANE), 1)
    row = lax.broadcasted_iota(I32, (QT, LANE), 0)

    keys_ref[0] = jnp.where(col < N_META, _sort_key(_idx_scores(iq, wb_ref, ikm_ref[...])), INT_MIN)

    def fill(j, carry):
        ikt = ik_ref[0, pl.ds(pl.multiple_of(j * QT, QT), QT), :]
        keys_ref[j + 1] = _sort_key(_idx_scores(iq, wb_ref, ikt))
        return carry

    lax.fori_loop(0, i, fill, 0)
    ikt = ik_ref[0, pl.ds(pl.multiple_of(i * QT, QT), QT), :]
    keys_ref[i + 1] = jnp.where(col <= row, _sort_key(_idx_scores(iq, wb_ref, ikt)), INT_MIN)

    t = _kth_largest_key(keys_ref, i + 2, float(n_top), QT)
    thr_ref[0] = _key_to_float(t)


def _attn_prompt_kernel(q_ref, iq_ref, iw_ref, thr_ref, k_ref, v_ref, ik_ref, km_ref, vm_ref, ikm_ref,
                        bt_ref, o_ref, m_ref, l_ref, acc_ref, wb_ref):
    i = pl.program_id(1)
    kb = pl.program_id(2)
    sub = KBLK // QT
    scale = ATT_DH ** -0.5
    col = lax.broadcasted_iota(I32, (QT, LANE), 1)
    row = lax.broadcasted_iota(I32, (QT, LANE), 0)

    def process(kt, vt, ikt, visible, bias_idx):
        q = q_ref[0]
        s = _idx_scores(iq_ref[0], wb_ref, ikt)
        sel = jnp.logical_and(s >= thr_ref[0], visible)
        for h in range(ATT_HEADS):
            sl = slice(h * ATT_DH, (h + 1) * ATT_DH)
            lg = lax.dot_general(q[:, sl], kt[:, sl], _NT, preferred_element_type=F32) * scale
            lg = jnp.where(sel, lg + bt_ref[bias_idx, h], NEG)
            m_old = m_ref[h]
            m_new = jnp.maximum(m_old, jnp.max(lg, axis=1, keepdims=True))
            alpha = jnp.exp(m_old - m_new)
            p = jnp.exp(lg - m_new)
            l_ref[h] = alpha * l_ref[h] + jnp.sum(p, axis=1, keepdims=True)
            acc_ref[:, sl] = alpha * acc_ref[:, sl] + jnp.dot(p.astype(BF), vt[:, sl],
                                                             preferred_element_type=F32)
            m_ref[h] = m_new

    @pl.when(kb == 0)
    def _():
        _fill_wb(wb_ref, iw_ref[0])
        m_ref[...] = jnp.full(m_ref.shape, NEG, F32)
        l_ref[...] = jnp.zeros(l_ref.shape, F32)
        acc_ref[...] = jnp.zeros(acc_ref.shape, F32)
        process(km_ref[...], vm_ref[...], ikm_ref[...], col < N_META, jnp.where(i == 0, 3, 2))

    nvis = jnp.clip(i - kb * sub + 1, 0, sub)

    def body(u, carry):
        j = kb * sub + u
        off = pl.multiple_of(u * QT, QT)
        kt = k_ref[0, pl.ds(off, QT), :]
        vt = v_ref[0, pl.ds(off, QT), :]
        ikt = ik_ref[0, pl.ds(off, QT), :]
        visible = (col + j * QT) <= (row + i * QT)
        process(kt, vt, ikt, visible, jnp.minimum(i - j, 2))
        return carry

    lax.fori_loop(0, nvis, body, 0)

    @pl.when(kb == pl.num_programs(2) - 1)
    def _():
        for h in range(ATT_HEADS):
            sl = slice(h * ATT_DH, (h + 1) * ATT_DH)
            o_ref[0, :, sl] = (acc_ref[:, sl] / l_ref[h]).astype(BF)


def _dsa_prompt(pr, me, rel_bias, b, s, n_top):
    assert s % KBLK == 0 and REL_MAX_DIST <= QT
    nq = s // QT
    nkb = s // KBLK
    sub = KBLK // QT
    r3 = lambda a: a.reshape(b, s, a.shape[-1])
    iq, iw, ikb = r3(pr["iq"]), r3(pr["iw"]), r3(pr["ikb"])
    aq, akb, avb = r3(pr["aq"]), r3(pr["akb"]), r3(pr["avb"])
    padm = lambda a: jnp.pad(a, ((0, QT - N_META), (0, 0)))
    ikm, km, vm = padm(me["ikb"]), padm(me["akb"]), padm(me["avb"])

    thr = pl.pallas_call(
        functools.partial(_thr_prompt_kernel, n_top=n_top), grid=(b, nq),
        in_specs=[pl.BlockSpec((1, QT, 1024), lambda bi, i: (bi, i, 0)),
                  pl.BlockSpec((1, QT, LANE), lambda bi, i: (bi, i, 0)),
                  pl.BlockSpec((1, s, LANE), lambda bi, i: (bi, 0, 0)),
                  pl.BlockSpec((QT, LANE), lambda bi, i: (0, 0))],
        out_specs=pl.BlockSpec((1, QT, 1), lambda bi, i: (bi, i, 0)),
        out_shape=jax.ShapeDtypeStruct((b, s, 1), F32),
        scratch_shapes=[pltpu.VMEM((nq + 1, QT, LANE), I32), pltpu.VMEM((IDX_HEADS, QT, LANE), F32)],
        compiler_params=_cparams(("parallel", "arbitrary")), name="dsa_thr_prompt",
    )(iq, iw, ikb, ikm)

    bb = _t5_bias_by_dist(rel_bias, 2 * QT)
    qi = jnp.arange(QT)[:, None]
    kj = jnp.arange(LANE)[None, :]
    bt = jnp.stack([bb[:, jnp.clip(qi - kj, 0, 2 * QT)],
                    bb[:, jnp.clip(QT + qi - kj, 0, 2 * QT)],
                    jnp.broadcast_to(bb[:, 2 * QT][:, None, None], (ATT_HEADS, QT, LANE)),
                    bb[:, jnp.clip(N_META + qi - kj, 0, 2 * QT)]])

    kmap = lambda bi, i, kb: (bi, jnp.minimum(kb, i // sub), 0)
    qmap = lambda bi, i, kb: (bi, i, 0)
    const2 = lambda bi, i, kb: (0, 0)
    return pl.pallas_call(
        _attn_prompt_kernel, grid=(b, nq, nkb),
        in_specs=[pl.BlockSpec((1, QT, 1024), qmap), pl.BlockSpec((1, QT, 1024), qmap),
                  pl.BlockSpec((1, QT, LANE), qmap), pl.BlockSpec((1, QT, 1), qmap),
                  pl.BlockSpec((1, KBLK, 1024), kmap), pl.BlockSpec((1, KBLK, 1024), kmap),
                  pl.BlockSpec((1, KBLK, LANE), kmap),
                  pl.BlockSpec((QT, 1024), const2), pl.BlockSpec((QT, 1024), const2),
                  pl.BlockSpec((QT, LANE), const2),
                  pl.BlockSpec(bt.shape, lambda bi, i, kb: (0, 0, 0, 0))],
        out_specs=pl.BlockSpec((1, QT, 1024), qmap),
        out_shape=jax.ShapeDtypeStruct((b, s, 1024), BF),
        scratch_shapes=[pltpu.VMEM((ATT_HEADS, QT, 1), F32), pltpu.VMEM((ATT_HEADS, QT, 1), F32),
                        pltpu.VMEM((QT, 1024), F32), pltpu.VMEM((IDX_HEADS, QT, LANE), F32)],
        compiler_params=_cparams(("parallel", "parallel", "arbitrary")), name="dsa_attn_prompt",
    )(aq, iq, iw, thr, akb, avb, ikb, km, vm, ikm, bt)


def _sample_scores(iq, iw, ikt):
    r = lax.dot_general(iq, ikt, _NT, preferred_element_type=F32)
    r = jnp.maximum(r, 0.0) * iw
    nq = r.shape[0] // IDX_HEADS
    s = r[0:nq]
    for h in range(1, IDX_HEADS):
        s = s + r[h * nq:(h + 1) * nq]
    return s + 0.0


def _thr_sample_kernel(pt_ref, iq_ref, iw_ref, ikn_ref, *rest, n_top, n_pages, s_new):
    pages = rest[:PG_THR]
    thr_ref, keys_ref = rest[PG_THR], rest[PG_THR + 1]
    g = pl.program_id(1)
    iq = iq_ref[0]
    iw = iw_ref[0]
    nq = iq.shape[0] // IDX_HEADS
    for u in range(PG_THR):
        keys_ref[g * PG_THR + u] = _sort_key(_sample_scores(iq, iw, pages[u][0].astype(BF)))

    @pl.when(g == pl.num_programs(1) - 1)
    def _():
        col = lax.broadcasted_iota(I32, (nq, LANE), 1)
        row = lax.broadcasted_iota(I32, (nq, LANE), 0)
        k = _sort_key(_sample_scores(iq, iw, ikn_ref[0]))
        keys_ref[n_pages] = jnp.where(jnp.logical_and(col <= row, col < s_new), k, INT_MIN)
        t = _kth_largest_key(keys_ref, n_pages + 1, float(n_top), nq)
        thr_ref[0] = _key_to_float(t)


def _attn_sample_kernel(pt_ref, q_ref, iq_ref, iw_ref, thr_ref, kn_ref, vn_ref, ikn_ref, bts_ref, btn_ref,
                        *rest, s_new):
    kp = rest[0:PG_ATT]
    vp = rest[PG_ATT:2 * PG_ATT]
    ip = rest[2 * PG_ATT:3 * PG_ATT]
    o_ref, qbd_ref, m_ref, l_ref, acc_ref = rest[3 * PG_ATT:]
    g = pl.program_id(1)
    last = pl.num_programs(1) - 1
    scale = ATT_DH ** -0.5
    nq = q_ref.shape[1]

    @pl.when(g == 0)
    def _():
        qbd_ref[...] = jnp.zeros(qbd_ref.shape, F32)
        for h in range(ATT_HEADS):
            sl = slice(h * ATT_DH, (h + 1) * ATT_DH)
            qbd_ref[h * nq:(h + 1) * nq, sl] = q_ref[0, :, sl].astype(F32)
        m_ref[...] = jnp.full(m_ref.shape, NEG, F32)
        l_ref[...] = jnp.zeros(l_ref.shape, F32)
        acc_ref[...] = jnp.zeros(acc_ref.shape, F32)

    iq = iq_ref[0]
    iw = iw_ref[0]
    thr = thr_ref[0]

    def update(lg, vt):
        m_old = m_ref[...]
        m_new = jnp.maximum(m_old, jnp.max(lg, axis=1, keepdims=True))
        alpha = jnp.exp(m_old - m_new)
        p = jnp.exp(lg - m_new)
        l_ref[...] = alpha * l_ref[...] + jnp.sum(p, axis=1, keepdims=True)
        acc_ref[...] = alpha * acc_ref[...] + jnp.dot(p.astype(BF), vt, preferred_element_type=F32)
        m_ref[...] = m_new

    qbd = qbd_ref[...].astype(BF)
    for u in range(PG_ATT):
        s = _sample_scores(iq, iw, ip[u][0].astype(BF))
        madd = jnp.where(s >= thr, 0.0, NEG)
        madd = jnp.concatenate([madd] * ATT_HEADS, axis=0)
        lg = lax.dot_general(qbd, kp[u][0].astype(BF), _NT, preferred_element_type=F32) * scale
        is_last_page = jnp.logical_and(g == last, u == PG_ATT - 1)
        lg = lg + bts_ref[jnp.where(is_last_page, 1, 0)] + madd
        update(lg, vp[u][0].astype(BF))

    @pl.when(g == last)
    def _():
        col = lax.broadcasted_iota(I32, (nq, LANE), 1)
        row = lax.broadcasted_iota(I32, (nq, LANE), 0)
        s = _sample_scores(iq, iw, ikn_ref[0])
        ok = jnp.logical_and(jnp.logical_and(col <= row, col < s_new), s >= thr)
        madd = jnp.where(ok, 0.0, NEG)[:, 0:NEW_PAD]
        madd = jnp.concatenate([madd] * ATT_HEADS, axis=0)
        lg = lax.dot_general(qbd, kn_ref[0], _NT, preferred_element_type=F32) * scale
        update(lg + btn_ref[...] + madd, vn_ref[0])
        for h in range(ATT_HEADS):
            sl = slice(h * ATT_DH, (h + 1) * ATT_DH)
            rs = slice(h * nq, (h + 1) * nq)
            o_ref[0, :, sl] = acc_ref[rs, sl] / l_ref[rs, :]


def _dsa_sample(sm, cache_k, cache_v, cache_kidx, page_table, rel_bias, db, sd):
    n_pages = page_table.shape[1]
    past = n_pages * PAGE_SIZE
    n_top = min(TOPK_MAX, (past + sd) // 4)
    assert n_pages % PG_THR == 0 and n_pages % PG_ATT == 0 and sd <= NEW_PAD and sd == 8
    rows = IDX_HEADS * sd
    iq = sm["iq"].reshape(db, sd, IDX_HEADS, LANE)[..., :IDX_DIM]
    iq = iq.transpose(0, 2, 1, 3).reshape(db, rows, IDX_DIM)
    iw = sm["iw"].reshape(db, sd, LANE)[..., :IDX_HEADS].transpose(0, 2, 1).reshape(db, rows, 1)
    ikn = jnp.pad(sm["ikb"].reshape(db, sd, LANE)[..., :IDX_DIM], ((0, 0), (0, LANE - sd), (0, 0)))
    pt = page_table.reshape(-1).astype(I32)

    def page_spec(width, per_step, u):
        return pl.BlockSpec((1, PAGE_SIZE, width),
                            lambda bi, g, pt_ref: (pt_ref[bi * n_pages + g * per_step + u], 0, 0))

    seq3 = lambda a: pl.BlockSpec((1,) + a.shape[1:], lambda bi, g, pt_ref: (bi, 0, 0))
    full = lambda a: pl.BlockSpec(a.shape, lambda bi, g, pt_ref: (0,) * a.ndim)

    thr = pl.pallas_call(
        functools.partial(_thr_sample_kernel, n_top=n_top, n_pages=n_pages, s_new=sd),
        grid_spec=pltpu.PrefetchScalarGridSpec(
            num_scalar_prefetch=1, grid=(db, n_pages // PG_THR),
            in_specs=[seq3(iq), seq3(iw), seq3(ikn)]
                     + [page_spec(IDX_DIM, PG_THR, u) for u in range(PG_THR)],
            out_specs=pl.BlockSpec((1, sd, 1), lambda bi, g, pt_ref: (bi, 0, 0)),
            scratch_shapes=[pltpu.VMEM((n_pages + 1, sd, LANE), I32)]),
        out_shape=jax.ShapeDtypeStruct((db, sd, 1), F32),
        compiler_params=_cparams(("parallel", "arbitrary")), name="dsa_thr_sample",
    )(pt, iq, iw, ikn, *([cache_kidx] * PG_THR))

    bb = _t5_bias_by_dist(rel_bias, 2 * PAGE_SIZE)
    qq = jnp.arange(sd)[:, None]
    tt = jnp.arange(PAGE_SIZE)[None, :]
    hq = lambda a: a.reshape(ATT_HEADS * sd, a.shape[-1])
    bts = jnp.stack([hq(jnp.broadcast_to(bb[:, 2 * PAGE_SIZE][:, None, None], (ATT_HEADS, sd, PAGE_SIZE))),
                     hq(bb[:, jnp.clip(PAGE_SIZE + qq - tt, 0, 2 * PAGE_SIZE)])])
    btn = hq(bb[:, jnp.clip(qq - jnp.arange(NEW_PAD)[None, :], 0, 2 * PAGE_SIZE)])
    q3 = sm["aq"].reshape(db, sd, 1024)
    padn = lambda a: jnp.pad(a.reshape(db, sd, 1024), ((0, 0), (0, NEW_PAD - sd), (0, 0)))
    kn, vn = padn(sm["akb"]), padn(sm["avb"])

    return pl.pallas_call(
        functools.partial(_attn_sample_kernel, s_new=sd),
        grid_spec=pltpu.PrefetchScalarGridSpec(
            num_scalar_prefetch=1, grid=(db, n_pages // PG_ATT),
            in_specs=[seq3(q3), seq3(iq), seq3(iw), seq3(thr), seq3(kn), seq3(vn), seq3(ikn),
                      full(bts), full(btn)]
                     + [page_spec(1024, PG_ATT, u) for u in range(PG_ATT)]
                     + [page_spec(1024, PG_ATT, u) for u in range(PG_ATT)]
                     + [page_spec(IDX_DIM, PG_ATT, u) for u in range(PG_ATT)],
            out_specs=pl.BlockSpec((1, sd, 1024), lambda bi, g, pt_ref: (bi, 0, 0)),
            scratch_shapes=[pltpu.VMEM((rows, 1024), F32), pltpu.VMEM((rows, 1), F32),
                            pltpu.VMEM((rows, 1), F32), pltpu.VMEM((rows, 1024), F32)]),
        out_shape=jax.ShapeDtypeStruct((db, sd, 1024), F32),
        compiler_params=_cparams(("parallel", "arbitrary")), name="dsa_attn_sample",
    )(pt, q3, iq, iw, thr, kn, vn, ikn, bts, btn,
      *([cache_k] * PG_ATT), *([cache_v] * PG_ATT), *([cache_kidx] * PG_ATT))


def _merge_kernel(x_ref, ro_ref, rg_ref, ao_ref, ga_ref, gb_ref, rng_ref, wr_ref, wa_ref, wo_ref, y_ref, yr_ref):
    for h in range(RET_HEADS):
        sl = slice(h * RET_DV, (h + 1) * RET_DV)
        rg = rg_ref[:, sl]
        yr_ref[:, sl] = (rg * jax.nn.sigmoid(rg) * _rms(ro_ref[:, sl], rng_ref[:, sl])).astype(BF)
    y_r = jnp.dot(yr_ref[...], wr_ref[...], preferred_element_type=F32)
    y_a = jnp.dot(ao_ref[...].astype(BF), wa_ref[...], preferred_element_type=F32)
    merged = jax.nn.sigmoid(ga_ref[...]) * y_r + jax.nn.sigmoid(gb_ref[...]) * y_a
    y_ref[...] = x_ref[...] + jnp.dot(merged.astype(BF), wo_ref[...], preferred_element_type=F32)


def _merge(x2d, ret_o, rg, att_o, ga, gb, ret_norm_g, wr, wa, wo):
    n = x2d.shape[0]
    t = min(TOK_TILE, n)
    assert n % t == 0
    row = pl.BlockSpec((t, 1024), lambda i: (i, 0))
    full = lambda a: pl.BlockSpec(a.shape, lambda i: (0,) * a.ndim)
    rng = ret_norm_g.reshape(1, RET_HEADS * RET_DV)
    return pl.pallas_call(
        _merge_kernel, grid=(n // t,),
        in_specs=[row] * 6 + [full(rng), full(wr), full(wa), full(wo)],
        out_specs=row, out_shape=jax.ShapeDtypeStruct((n, 1024), F32),
        scratch_shapes=[pltpu.VMEM((t, 1024), BF)],
        compiler_params=_cparams(("parallel",)), name="merge",
    )(x2d, ret_o, rg, att_o, ga, gb, rng, wr, wa, wo)


def _peer_route_kernel(x_ref, g_ref, wq_ref, sk1_ref, sk2_ref, hn_ref, s1_ref, s2_ref, st_ref,
                       qt_ref, v1_ref, v2_ref, c_ref, sc_ref):
    hn = _rms(x_ref[...], g_ref[...]).astype(BF)
    hn_ref[...] = hn
    qt_ref[...] = lax.dot_general(wq_ref[...], hn, _NT, preferred_element_type=F32)
    half = PEER_DQ // 2
    ninf = -jnp.inf

    def top_vals(src_ref, h, dst_ref):
        x = src_ref[h]
        for r in range(PEER_TOPK):
            v = jnp.max(x, axis=0, keepdims=True)
            dst_ref[r:r + 1, :] = v
            x = jnp.where(x == v, ninf, x)

    def head(h, carry):
        o = pl.multiple_of(h * PEER_DQ, PEER_DQ)
        q1 = qt_ref[pl.ds(o, half), :].astype(BF)
        q2 = qt_ref[pl.ds(o + half, half), :].astype(BF)
        s1_ref[h] = jnp.dot(sk1_ref[h], q1, preferred_element_type=F32)
        s2_ref[h] = jnp.dot(sk2_ref[h], q2, preferred_element_type=F32)
        top_vals(s1_ref, h, v1_ref)
        top_vals(s2_ref, h, v2_ref)
        v2 = v2_ref[...]
        for a in range(PEER_TOPK):
            c_ref[a * PEER_TOPK:(a + 1) * PEER_TOPK, :] = v1_ref[a:a + 1, :] + v2
        for r in range(PEER_TOPK):
            c = c_ref[...]
            v = jnp.max(c, axis=0, keepdims=True)
            sc_ref[r:r + 1, :] = v
            c_ref[...] = jnp.where(c == v, ninf, c)
        sc = sc_ref[...]
        z = jnp.sum(jnp.exp(sc - sc[0:1, :]), axis=0, keepdims=True)
        st_ref[h, 0:1, :] = sc[PEER_TOPK - 1:PEER_TOPK, :]
        st_ref[h, 1:2, :] = v1_ref[0:1, :]
        st_ref[h, 2:3, :] = v2_ref[0:1, :]
        st_ref[h, 3:4, :] = 1.0 / z
        st_ref[h, 4:8, :] = jnp.zeros((4, z.shape[1]), F32)
        return carry

    lax.fori_loop(0, PEER_HEADS, head, 0)


def _peer_expert_kernel(hn_ref, x_ref, s1_ref, s2_ref, st_ref, u_ref, vt_ref, y_ref,
                        e1_ref, e2_ref, act_ref, w_ref, out_ref):
    c = pl.program_id(1)
    slabs = EXP_CHUNK // PEER_KEYS

    @pl.when(c == 0)
    def _():
        for h in range(PEER_HEADS):
            st = st_ref[h]
            e1_ref[h] = jnp.exp(s1_ref[h] - st[1:2, :]) * st[3:4, :]
            e2_ref[h] = jnp.exp(s2_ref[h] - st[2:3, :])
        out_ref[...] = jnp.zeros(out_ref.shape, F32)

    act_ref[...] = lax.dot_general(u_ref[...], hn_ref[...], _NT, preferred_element_type=F32)

    def slab(al, carry):
        a = c * slabs + al
        g = None
        for h in range(PEER_HEADS):
            ssum = s1_ref[h, pl.ds(a, 1), :] + s2_ref[h]
            gh = jnp.where(ssum >= st_ref[h, 0:1, :], e2_ref[h], 0.0) * e1_ref[h, pl.ds(a, 1), :]
            g = gh if g is None else g + gh
        o = pl.multiple_of(al * PEER_KEYS, PEER_KEYS)
        x = act_ref[pl.ds(o, PEER_KEYS), :]
        gelu = 0.5 * x * (1.0 + lax.erf(x * math.sqrt(0.5)))
        w_ref[pl.ds(o, PEER_KEYS), :] = (g * gelu).astype(BF)
        return carry

    lax.fori_loop(0, slabs, slab, 0)
    out_ref[...] += jnp.dot(vt_ref[...], w_ref[...], preferred_element_type=F32)

    @pl.when(c == pl.num_programs(1) - 1)
    def _():
        y_ref[...] = x_ref[...] + out_ref[...].T


def _peer(x2d, norm_g, wq_t, sk1, sk2, u_bf, vt_bf):
    n = x2d.shape[0]
    t = min(TOK_TILE, n)
    assert n % t == 0 and t % LANE == 0
    nt = n // t
    n_exp = u_bf.shape[0]
    assert n_exp == PEER_KEYS * PEER_KEYS and n_exp % EXP_CHUNK == 0
    g = norm_g.reshape(1, D_MODEL)
    row = lambda width: pl.BlockSpec((t, width), lambda i: (i, 0))
    full = lambda a: pl.BlockSpec(a.shape, lambda i: (0,) * a.ndim)
    tk = pl.BlockSpec((PEER_HEADS, PEER_KEYS, t), lambda i: (0, 0, i))
    hn, s1, s2, st = pl.pallas_call(
        _peer_route_kernel, grid=(nt,),
        in_specs=[row(D_MODEL), full(g), full(wq_t), full(sk1), full(sk2)],
        out_specs=[row(D_MODEL), tk, tk, pl.BlockSpec((PEER_HEADS, 8, t), lambda i: (0, 0, i))],
        out_shape=[jax.ShapeDtypeStruct((n, D_MODEL), BF),
                   jax.ShapeDtypeStruct((PEER_HEADS, PEER_KEYS, n), F32),
                   jax.ShapeDtypeStruct((PEER_HEADS, PEER_KEYS, n), F32),
                   jax.ShapeDtypeStruct((PEER_HEADS, 8, n), F32)],
        scratch_shapes=[pltpu.VMEM((PEER_HEADS * PEER_DQ, t), F32),
                        pltpu.VMEM((PEER_TOPK, t), F32), pltpu.VMEM((PEER_TOPK, t), F32),
                        pltpu.VMEM((PEER_TOPK * PEER_TOPK, t), F32), pltpu.VMEM((PEER_TOPK, t), F32)],
        compiler_params=_cparams(("parallel",)), name="peer_route",
    )(x2d, g, wq_t, sk1, sk2)

    tk2 = pl.BlockSpec((PEER_HEADS, PEER_KEYS, t), lambda i, c: (0, 0, i))
    row2 = lambda width: pl.BlockSpec((t, width), lambda i, c: (i, 0))
    return pl.pallas_call(
        _peer_expert_kernel, grid=(nt, n_exp // EXP_CHUNK),
        in_specs=[row2(D_MODEL), row2(D_MODEL), tk2, tk2,
                  pl.BlockSpec((PEER_HEADS, 8, t), lambda i, c: (0, 0, i)),
                  pl.BlockSpec((EXP_CHUNK, D_MODEL), lambda i, c: (c, 0)),
                  pl.BlockSpec((D_MODEL, EXP_CHUNK), lambda i, c: (0, c))],
        out_specs=row2(D_MODEL), out_shape=jax.ShapeDtypeStruct((n, D_MODEL), F32),
        scratch_shapes=[pltpu.VMEM((PEER_HEADS, PEER_KEYS, t), F32), pltpu.VMEM((PEER_HEADS, PEER_KEYS, t), F32),
                        pltpu.VMEM((EXP_CHUNK, t), F32), pltpu.VMEM((EXP_CHUNK, t), BF),
                        pltpu.VMEM((D_MODEL, t), F32)],
        compiler_params=_cparams(("parallel", "arbitrary")), name="peer_expert",
    )(hn, x2d, s1, s2, st, u_bf, vt_bf)


def _rope_tables(pos):
    half = RET_DK // 2
    inv = ROPE_BASE ** (-jnp.arange(half, dtype=F32) / half)
    ang = pos.astype(F32)[:, None] * inv[None, :]
    cos, sin = jnp.cos(ang), jnp.sin(ang)
    return jnp.concatenate([cos, cos], axis=1), jnp.concatenate([-sin, sin], axis=1)


def _split_w_in(w):
    sizes = (RET_HEADS * RET_DK, RET_HEADS * RET_DK, RET_HEADS * RET_DV, RET_HEADS * RET_DV,
             ATT_HEADS * ATT_DH, ATT_HEADS * ATT_DH, ATT_HEADS * ATT_DH,
             IDX_HEADS * IDX_DIM, IDX_DIM, IDX_HEADS, D_MODEL, D_MODEL)
    assert w.shape[1] == sum(sizes)
    parts, o = [], 0
    for s in sizes:
        parts.append(w[:, o:o + s])
        o += s
    rq, rk, rv, rg, aq, ak, av, iq, ik, iw, ga, gb = parts
    d = w.shape[0]
    iq = jnp.pad(iq.reshape(d, IDX_HEADS, IDX_DIM), ((0, 0), (0, 0), (0, LANE - IDX_DIM))).reshape(d, -1)
    ik = jnp.pad(ik, ((0, 0), (0, LANE - IDX_DIM)))
    iw = jnp.pad(iw, ((0, 0), (0, LANE - IDX_HEADS)))
    cat = lambda xs: jnp.concatenate(xs, axis=1).astype(BF)
    return cat([rq, rk, rv, rg]), cat([aq, ak, av]), cat([iq, ik, iw, ga, gb])


def kernel(x_prompt, x_sample, cache_k, cache_v, cache_kidx, state_ret, page_table, meta_tokens, rel_bias,
           attn_norm_g, w_in, q_norm_g, k_norm_g, ret_norm_g, w_ret_o, w_att_o, w_out, ffn_norm_g,
           peer_w_q, peer_sub_keys, peer_u, peer_v):
    b, s, d = x_prompt.shape
    db, sd, _ = x_sample.shape
    assert w_in.shape[0] == 1 and d == D_MODEL and s % RET_CHUNK == 0
    n_pages = page_table.shape[1]
    past = n_pages * PAGE_SIZE
    n_top_p = min(TOPK_MAX, s // 4)
    l = 0

    wts = _split_w_in(w_in[l])
    qg = q_norm_g[l].reshape(1, ATT_DH)
    kg = k_norm_g[l].reshape(1, ATT_DH)
    wr, wa, wo = w_ret_o[l].astype(BF), w_att_o[l].astype(BF), w_out[l].astype(BF)
    wq_t = peer_w_q[l].T.astype(BF)
    sk1 = peer_sub_keys[l][:, 0].astype(BF)
    sk2 = peer_sub_keys[l][:, 1].astype(BF)
    u_bf = peer_u[l].astype(BF)
    vt_bf = peer_v[l].T.astype(BF)
    hkv = (ATT_HEADS, ATT_DH)

    xr = x_prompt.reshape(b * s, d)
    pr = _in_proj(xr, attn_norm_g[l], wts, *_rope_tables(N_META + jnp.arange(s)), qg, kg)
    me = _in_proj(meta_tokens.astype(x_prompt.dtype), attn_norm_g[l], wts, *_rope_tables(jnp.arange(N_META)),
                  qg, kg)
    m3 = lambda a: a.reshape(1, N_META, a.shape[-1])
    _, s_meta = _retention(m3(me["rq"]), m3(me["rk"]), m3(me["rv"]),
                           jnp.zeros((1, RET_HEADS, RET_DK, RET_DV), F32), N_META)
    r3 = lambda a: a.reshape(b, s, a.shape[-1])
    ret_o, s_fin = _retention(r3(pr["rq"]), r3(pr["rk"]), r3(pr["rv"]), s_meta, RET_CHUNK)
    att_o = _dsa_prompt(pr, me, rel_bias, b, s, n_top_p)
    x1 = _merge(xr, ret_o.reshape(b * s, -1), pr["rg"], att_o.reshape(b * s, -1), pr["ga"], pr["gb"],
                ret_norm_g[l], wr, wa, wo)
    y_prompt = _peer(x1, ffn_norm_g[l], wq_t, sk1, sk2, u_bf, vt_bf).reshape(b, s, d)

    def with_meta(real, meta):
        meta = jnp.broadcast_to(meta[None], (b,) + meta.shape)
        return jnp.concatenate([meta, real.reshape(b, s, -1)], axis=1)

    k_prompt = with_meta(pr["ak"], me["ak"]).reshape(1, b, s + N_META, *hkv).astype(cache_k.dtype)
    v_prompt = with_meta(pr["av"], me["av"]).reshape(1, b, s + N_META, *hkv).astype(cache_v.dtype)
    kidx_prompt = with_meta(pr["ik"], me["ik"])[None].astype(cache_kidx.dtype)
    ret_state_prompt = s_fin[None].astype(state_ret.dtype)

    xs = x_sample.reshape(db * sd, d)
    pos_s = jnp.tile(past + jnp.arange(sd), (db * sd) // sd)
    sm = _in_proj(xs, attn_norm_g[l], wts, *_rope_tables(pos_s), qg, kg)
    s3 = lambda a: a.reshape(db, sd, a.shape[-1])
    ret_os, s_new = _retention(s3(sm["rq"]), s3(sm["rk"]), s3(sm["rv"]), state_ret[l].astype(F32), sd)
    n_pool = cache_k.shape[1]
    att_os = _dsa_sample(sm, cache_k[l].reshape(n_pool, PAGE_SIZE, -1), cache_v[l].reshape(n_pool, PAGE_SIZE, -1),
                         cache_kidx[l], page_table, rel_bias, db, sd)
    x1s = _merge(xs, ret_os.reshape(db * sd, -1), sm["rg"], att_os.reshape(db * sd, -1), sm["ga"], sm["gb"],
                 ret_norm_g[l], wr, wa, wo)
    y_sample = _peer(x1s, ffn_norm_g[l], wq_t, sk1, sk2, u_bf, vt_bf).reshape(db, sd, d)

    k_sample = sm["ak"].reshape(1, db, sd, *hkv).astype(cache_k.dtype)
    v_sample = sm["av"].reshape(1, db, sd, *hkv).astype(cache_v.dtype)
    kidx_sample = sm["ik"].reshape(1, db, sd, IDX_DIM).astype(cache_kidx.dtype)
    ret_state_sample = s_new[None].astype(state_ret.dtype)

    return (y_prompt, y_sample, k_prompt, v_prompt, kidx_prompt, ret_state_prompt,
            k_sample, v_sample, kidx_sample, ret_state_sample)
```

```python
import functools
import math

import jax
import jax.numpy as jnp
from jax import lax
from jax.experimental import pallas as pl
from jax.experimental.pallas import tpu as pltpu

F32 = jnp.float32
BF = jnp.bfloat16
I32 = jnp.int32

D_MODEL = 1024
PAGE_SIZE = 128
N_META = 16
RET_HEADS = 4
RET_DK = 128
RET_DV = 256
RET_CHUNK = 128
ROPE_BASE = 10000.0
ATT_HEADS = 8
ATT_DH = 128
IDX_HEADS = 8
IDX_DIM = 64
TOPK_MAX = 256
REL_BUCKETS = 32
REL_MAX_DIST = 128
PEER_HEADS = 8
PEER_KEYS = 128
PEER_DQ = 128
PEER_TOPK = 16
EPS = 1e-6
NEG = -1e30

LANE = 128
TOK_TILE = 512
QT = 128
KBLK_MAX = 1024
KEY_CHUNK = 512
EXP_CHUNK = 1024
PG_THR = 8
PG_ATT = 4
NEW_PAD = 16
VMEM_LIMIT = 52 * 1024 * 1024
INT_MIN = -2 ** 31
KEY_NEG_INF = -2139095041

_NT = (((1,), (1,)), ((), ()))
_TN = (((0,), (0,)), ((), ()))


def _cparams(sem):
    return pltpu.CompilerParams(dimension_semantics=sem, vmem_limit_bytes=VMEM_LIMIT)


def _rms(x, g):
    return x * lax.rsqrt(jnp.mean(x * x, axis=-1, keepdims=True) + EPS) * g


def _sort_key(s):
    b = lax.bitcast_convert_type(s, I32)
    return b ^ ((b >> 31) & 0x7FFFFFFF)


def _proj_ret_kernel(x_ref, g_ref, w_ref, cos_ref, sin_ref, rq_ref, rk_ref, rv_ref, rg_ref):
    h = _rms(x_ref[...], g_ref[...]).astype(BF)
    cosf = cos_ref[...]
    sinf = sin_ref[...]
    nqk = RET_HEADS * RET_DK
    for off, ref, sc in ((0, rq_ref, None), (nqk, rk_ref, RET_DK ** -0.5)):
        z = jnp.dot(h, w_ref[:, off:off + nqk], preferred_element_type=F32)
        for hd in range(RET_HEADS):
            sl = slice(hd * RET_DK, (hd + 1) * RET_DK)
            xh = z[:, sl]
            r = xh * cosf + pltpu.roll(xh, RET_DK // 2, 1) * sinf
            ref[:, sl] = r if sc is None else r * sc
    nv = RET_HEADS * RET_DV
    rv_ref[...] = jnp.dot(h, w_ref[:, 2 * nqk:2 * nqk + nv], preferred_element_type=F32)
    rg_ref[...] = jnp.dot(h, w_ref[:, 2 * nqk + nv:2 * nqk + 2 * nv], preferred_element_type=F32)


def _proj_att_kernel(x_ref, g_ref, w_ref, qg_ref, kg_ref, aq_ref, ak_ref, av_ref, akb_ref, avb_ref, *maybe_avt_ref):
    h = _rms(x_ref[...], g_ref[...]).astype(BF)
    n = ATT_HEADS * ATT_DH
    q = jnp.dot(h, w_ref[:, 0:n], preferred_element_type=F32)
    k = jnp.dot(h, w_ref[:, n:2 * n], preferred_element_type=F32)
    for hd in range(ATT_HEADS):
        sl = slice(hd * ATT_DH, (hd + 1) * ATT_DH)
        aq_ref[:, sl] = _rms(q[:, sl], qg_ref[...]).astype(BF)
        kn = _rms(k[:, sl], kg_ref[...])
        ak_ref[:, sl] = kn
        akb_ref[:, sl] = kn.astype(BF)
    v = jnp.dot(h, w_ref[:, 2 * n:3 * n], preferred_element_type=F32)
    av_ref[...] = v
    avb_ref[...] = v.astype(BF)
    for avt_ref in maybe_avt_ref:
        avt_ref[0] = v.T.astype(BF)


def _proj_idx_kernel(x_ref, g_ref, w_ref, iq_ref, ik_ref, ikb_ref, iw_ref, ga_ref, gb_ref):
    h = _rms(x_ref[...], g_ref[...]).astype(BF)
    nq = IDX_HEADS * LANE
    iq = jnp.dot(h, w_ref[:, 0:nq], preferred_element_type=F32)
    iq_ref[...] = (iq * IDX_DIM ** -0.5).astype(BF)
    ik = jnp.dot(h, w_ref[:, nq:nq + LANE], preferred_element_type=F32)
    ik_ref[...] = ik[:, 0:IDX_DIM]
    ikb_ref[...] = ik.astype(BF)
    iw = jnp.dot(h, w_ref[:, nq + LANE:nq + 2 * LANE], preferred_element_type=F32)
    iw_ref[...] = iw * IDX_HEADS ** -0.5
    o = nq + 2 * LANE
    ga_ref[...] = jnp.dot(h, w_ref[:, o:o + D_MODEL], preferred_element_type=F32)
    gb_ref[...] = jnp.dot(h, w_ref[:, o + D_MODEL:o + 2 * D_MODEL], preferred_element_type=F32)


def _in_proj(x2d, norm_g, wts, cosf, sinf, qg, kg, vt_seq_len=None):
    n = x2d.shape[0]
    t = min(TOK_TILE, n)
    assert n % t == 0 and cosf.shape[0] % t == 0
    nt = n // t
    ptiles = cosf.shape[0] // t
    w1, w2, w3 = wts
    row = lambda width: pl.BlockSpec((t, width), lambda i: (i, 0))
    full = lambda a: pl.BlockSpec(a.shape, lambda i: (0,) * a.ndim)
    tab = pl.BlockSpec((t, LANE), lambda i: (i % ptiles, 0))
    sds = lambda width, dt: jax.ShapeDtypeStruct((n, width), dt)
    g = norm_g.reshape(1, D_MODEL)
    cp = _cparams(("parallel",))

    rq, rk, rv, rg = pl.pallas_call(
        _proj_ret_kernel, grid=(nt,),
        in_specs=[row(D_MODEL), full(g), full(w1), tab, tab],
        out_specs=[row(512), row(512), row(1024), row(1024)],
        out_shape=[sds(512, F32), sds(512, F32), sds(1024, F32), sds(1024, F32)],
        compiler_params=cp, name="proj_ret")(x2d, g, w1, cosf, sinf)

    att_specs = [row(1024)] * 5
    att_shapes = [sds(1024, BF), sds(1024, F32), sds(1024, F32), sds(1024, BF), sds(1024, BF)]
    if vt_seq_len is not None:
        assert vt_seq_len % t == 0 and n % vt_seq_len == 0 and t % LANE == 0
        tps = vt_seq_len // t
        att_specs.append(pl.BlockSpec((1, 1024, t), lambda i: (i // tps, 0, i % tps)))
        att_shapes.append(jax.ShapeDtypeStruct((n // vt_seq_len, 1024, vt_seq_len), BF))
    att_out = pl.pallas_call(
        _proj_att_kernel, grid=(nt,),
        in_specs=[row(D_MODEL), full(g), full(w2), full(qg), full(kg)],
        out_specs=att_specs, out_shape=att_shapes,
        compiler_params=cp, name="proj_att")(x2d, g, w2, qg, kg)
    aq, ak, av, akb, avb = att_out[:5]
    avt = att_out[5] if vt_seq_len is not None else None

    iq, ik, ikb, iw, ga, gb = pl.pallas_call(
        _proj_idx_kernel, grid=(nt,),
        in_specs=[row(D_MODEL), full(g), full(w3)],
        out_specs=[row(1024), row(IDX_DIM), row(LANE), row(LANE), row(1024), row(1024)],
        out_shape=[sds(1024, BF), sds(IDX_DIM, F32), sds(LANE, BF), sds(LANE, F32),
                   sds(1024, F32), sds(1024, F32)],
        compiler_params=cp, name="proj_idx")(x2d, g, w3)

    return dict(rq=rq, rk=rk, rv=rv, rg=rg, aq=aq, ak=ak, av=av, akb=akb, avb=avb, avt=avt,
                iq=iq, ik=ik, ikb=ikb, iw=iw, ga=ga, gb=gb)


def _ret_kernel(q_ref, k_ref, v_ref, s0_ref, dmat_ref, dq_ref, dk_ref, gc_ref, o_ref, sout_ref, st_ref):
    c = pl.program_id(1)

    @pl.when(c == 0)
    def _():
        st_ref[...] = s0_ref[0]

    for h in range(RET_HEADS):
        qh = q_ref[0, :, h * RET_DK:(h + 1) * RET_DK].astype(BF)
        kh = k_ref[0, :, h * RET_DK:(h + 1) * RET_DK]
        vh = v_ref[0, :, h * RET_DV:(h + 1) * RET_DV].astype(BF)
        s_old = st_ref[h]
        sc = lax.dot_general(qh, kh.astype(BF), _NT, preferred_element_type=F32) * dmat_ref[h]
        intra = jnp.dot(sc.astype(BF), vh, preferred_element_type=F32)
        inter = jnp.dot(qh, s_old.astype(BF), preferred_element_type=F32) * dq_ref[h]
        o_ref[0, :, h * RET_DV:(h + 1) * RET_DV] = intra + inter
        kd = (kh * dk_ref[h]).astype(BF)
        upd = lax.dot_general(kd, vh, _TN, preferred_element_type=F32)
        st_ref[h] = s_old * gc_ref[h] + upd

    @pl.when(c == pl.num_programs(1) - 1)
    def _():
        sout_ref[0] = st_ref[...]


def _retention(q, k, v, s0, chunk):
    b, t, _ = q.shape
    assert t % chunk == 0
    nc = t // chunk
    ld = jnp.log1p(-jnp.exp2(-5.0 - jnp.arange(RET_HEADS, dtype=F32)))
    i = jnp.arange(chunk, dtype=F32)
    diff = i[:, None] - i[None, :]
    dmat = jnp.where(diff >= 0, jnp.exp(ld[:, None, None] * jnp.maximum(diff, 0.0)[None]), 0.0)
    dq = jnp.exp((i[None, :] + 1.0) * ld[:, None])[..., None]
    dk = jnp.exp((chunk - 1.0 - i)[None, :] * ld[:, None])[..., None]
    gc = jnp.exp(chunk * ld).reshape(RET_HEADS, 1, 1)
    shared = s0.shape[0] == 1
    full = lambda a: pl.BlockSpec(a.shape, lambda bi, ci: (0,) * a.ndim)
    seq = lambda width: pl.BlockSpec((1, chunk, width), lambda bi, ci: (bi, ci, 0))
    st_spec = pl.BlockSpec((1, RET_HEADS, RET_DK, RET_DV),
                           (lambda bi, ci: (0, 0, 0, 0)) if shared else (lambda bi, ci: (bi, 0, 0, 0)))
    return pl.pallas_call(
        _ret_kernel, grid=(b, nc),
        in_specs=[seq(RET_HEADS * RET_DK), seq(RET_HEADS * RET_DK), seq(RET_HEADS * RET_DV),
                  st_spec, full(dmat), full(dq), full(dk), full(gc)],
        out_specs=[seq(RET_HEADS * RET_DV),
                   pl.BlockSpec((1, RET_HEADS, RET_DK, RET_DV), lambda bi, ci: (bi, 0, 0, 0))],
        out_shape=[jax.ShapeDtypeStruct((b, t, RET_HEADS * RET_DV), F32),
                   jax.ShapeDtypeStruct((b, RET_HEADS, RET_DK, RET_DV), F32)],
        scratch_shapes=[pltpu.VMEM((RET_HEADS, RET_DK, RET_DV), F32)],
        compiler_params=_cparams(("parallel", "arbitrary")), name="retention",
    )(q, k, v, s0, dmat, dq, dk, gc)


def _t5_bias_by_dist(rel_bias, maxd):
    d = jnp.arange(maxd + 1, dtype=I32)
    max_exact = REL_BUCKETS // 2
    df = jnp.maximum(d, 1).astype(F32)
    large = max_exact + (jnp.log(df / max_exact) / math.log(REL_MAX_DIST / max_exact)
                         * (REL_BUCKETS - max_exact)).astype(I32)
    large = jnp.minimum(large, REL_BUCKETS - 1)
    bucket = jnp.where(d < max_exact, d, large)
    return rel_bias[bucket].T.astype(F32)


def _idx_scores_t(ikt, iqt, w):
    nq = iqt.shape[1] // IDX_HEADS
    r = jnp.dot(ikt, iqt, preferred_element_type=F32)
    s = None
    for h in range(IDX_HEADS):
        rh = jnp.maximum(r[:, h * nq:(h + 1) * nq], 0.0) * w[h:h + 1, :]
        s = rh if s is None else s + rh
    return s + 0.0


def _kth_largest_key(keys_ref, ntiles, n_top, rows):
    def step(sidx, t):
        bitv = jnp.left_shift(jnp.int32(1), 31 - sidx)
        cand = ((t ^ INT_MIN) | bitv) ^ INT_MIN
        cb = jnp.broadcast_to(cand, (rows, LANE))

        def count(j, acc):
            return acc + jnp.where(keys_ref[j] >= cb, 1.0, 0.0)

        acc = lax.fori_loop(0, ntiles, count, jnp.zeros((rows, LANE), F32))
        cnt = jnp.sum(acc, axis=1, keepdims=True)
        return jnp.where(cnt >= n_top, cand, t)

    t = lax.fori_loop(0, 32, step, jnp.full((rows, 1), INT_MIN, I32))
    return jnp.maximum(t, KEY_NEG_INF)


def _select_prompt_kernel(iq_ref, iwt_ref, ik_ref, ikm_ref, ltri_ref, madd_ref, keys_ref, *, n_top, seq, fill_kb):
    i = pl.program_id(1)
    nq = pl.num_programs(1)
    iqt = iq_ref[0, 0]
    w = iwt_ref[0]
    krow = lax.broadcasted_iota(I32, (QT, LANE), 0)
    qcol = lax.broadcasted_iota(I32, (QT, LANE), 1)
    big = 1e9

    keys_ref[0] = jnp.where(krow < N_META, _sort_key(_idx_scores_t(ikm_ref[...], iqt, w)), INT_MIN)
    per_blk = fill_kb // QT
    nblk = (i + per_blk) // per_blk

    def fill(blk, carry):
        s = _idx_scores_t(ik_ref[0, pl.ds(pl.multiple_of(blk * fill_kb, fill_kb), fill_kb), :], iqt, w)
        for u in range(per_blk):
            j = blk * per_blk + u
            visible = (krow + j * QT) <= (qcol + i * QT)
            keys_ref[j + 1] = jnp.where(visible, _sort_key(s[u * QT:(u + 1) * QT]), INT_MIN)
        return carry

    lax.fori_loop(0, nblk, fill, 0)
    nslots = nblk * per_blk + 1

    def count(pred):
        def add(j, acc):
            return acc + jnp.where(pred(keys_ref[j]), 1.0, 0.0)
        return jnp.sum(lax.fori_loop(0, nslots, add, jnp.zeros((QT, LANE), F32)), axis=0, keepdims=True)

    def cond(st):
        return jnp.logical_and(st[0] < 32, st[3] > 0)

    def step(st):
        sidx, t, cnt_t, _ = st
        bitv = jnp.left_shift(jnp.int32(1), 31 - sidx)
        cand = ((t ^ INT_MIN) | bitv) ^ INT_MIN
        cnt = count(lambda k: k >= cand)
        take = cnt >= n_top
        t = jnp.where(take, cand, t)
        cnt_t = jnp.where(take, cnt, cnt_t)
        return sidx + 1, t, cnt_t, (jnp.max(cnt_t) > n_top).astype(I32)

    _, t, cnt_t, _ = lax.while_loop(
        cond, step, (jnp.int32(0), jnp.full((1, LANE), INT_MIN, I32), jnp.full((1, LANE), big, F32), jnp.int32(1)))
    t = jnp.maximum(t, KEY_NEG_INF)
    has_ties = jnp.max(jnp.where(cnt_t < 0.5 * big, cnt_t, 0.0)) > n_top

    def store(slot, madd):
        @pl.when(slot == 0)
        def _():
            madd_ref[0, 0, seq:seq + QT, :] = madd

        @pl.when(slot > 0)
        def _():
            madd_ref[0, 0, pl.ds(pl.multiple_of((slot - 1) * QT, QT), QT), :] = madd

    @pl.when(jnp.logical_not(has_ties))
    def _():
        def emit(slot, carry):
            store(slot, jnp.where(keys_ref[slot] >= t, 0.0, NEG))
            return carry
        lax.fori_loop(0, nslots, emit, 0)

    @pl.when(has_ties)
    def _():
        need = n_top - count(lambda k: k > t)

        def emit(slot, seen):
            k = keys_ref[slot]
            eqf = jnp.where(k == t, 1.0, 0.0)
            rank = seen + jnp.dot(ltri_ref[...], eqf.astype(BF), preferred_element_type=F32)
            tie_ok = jnp.where(jnp.logical_and(k == t, rank < need), 0.0, NEG)
            store(slot, jnp.where(k > t, 0.0, tie_ok))
            return seen + jnp.sum(eqf, axis=0, keepdims=True)
        lax.fori_loop(0, nslots, emit, jnp.zeros((1, LANE), F32))

    def blank(j, carry):
        madd_ref[0, 0, pl.ds(pl.multiple_of(j * QT, QT), QT), :] = jnp.full((QT, LANE), NEG, F32)
        return carry

    lax.fori_loop(nslots - 1, nq, blank, 0)


def _attn_prompt_kernel(q_ref, k_ref, vt_ref, madd_ref, km_ref, vtm_ref, maddm_ref, bt_ref, o_ref,
                        m_ref, l_ref, acc_ref, *, sub, pc):
    i = pl.program_id(1)
    kb = pl.program_id(2)
    scale = ATT_DH ** -0.5

    def process(kt, vtt, madd, bias_idx):
        q = q_ref[0]
        heads = [slice(h * ATT_DH, (h + 1) * ATT_DH) for h in range(ATT_HEADS)]
        lgs = [lax.dot_general(kt[:, sl], q[:, sl], _NT, preferred_element_type=F32) for sl in heads]
        ps, alphas = [], []
        for h in range(ATT_HEADS):
            bias = jnp.concatenate([bt_ref[bi, h] for bi in bias_idx], axis=0)
            lg = lgs[h] * scale + bias + madd
            m_old = m_ref[h:h + 1, :]
            m_new = jnp.maximum(m_old, jnp.max(lg, axis=0, keepdims=True))
            alpha = jnp.exp(m_old - m_new)
            p = jnp.exp(lg - m_new)
            l_ref[h:h + 1, :] = alpha * l_ref[h:h + 1, :] + jnp.sum(p, axis=0, keepdims=True)
            m_ref[h:h + 1, :] = m_new
            ps.append(p.astype(BF))
            alphas.append(alpha)
        for h, sl in enumerate(heads):
            acc_ref[sl, :] = alphas[h] * acc_ref[sl, :] + jnp.dot(vtt[sl, :], ps[h], preferred_element_type=F32)

    @pl.when(kb == 0)
    def _():
        m_ref[...] = jnp.full(m_ref.shape, NEG, F32)
        l_ref[...] = jnp.zeros(l_ref.shape, F32)
        acc_ref[...] = jnp.zeros(acc_ref.shape, F32)
        process(km_ref[...], vtm_ref[...], maddm_ref[0, 0], [jnp.where(i == 0, 3, 2)])

    nchunk = jnp.clip((i - kb * sub + pc) // pc, 0, sub // pc)

    def body(c, carry):
        off = pl.multiple_of(c * (pc * QT), pc * QT)
        j0 = kb * sub + c * pc
        process(k_ref[0, pl.ds(off, pc * QT), :], vt_ref[0, :, pl.ds(off, pc * QT)],
                madd_ref[0, 0, pl.ds(off, pc * QT), :], [jnp.clip(i - (j0 + u), 0, 2) for u in range(pc)])
        return carry

    lax.fori_loop(0, nchunk, body, 0)

    @pl.when(kb == pl.num_programs(2) - 1)
    def _():
        for h in range(ATT_HEADS):
            sl = slice(h * ATT_DH, (h + 1) * ATT_DH)
            o_ref[0, :, sl] = (acc_ref[sl, :] * (1.0 / l_ref[h:h + 1, :])).T.astype(BF)


def _dsa_prompt(pr, me, rel_bias, b, s, n_top):
    kblk_sz = min(KBLK_MAX, s)
    assert s % kblk_sz == 0 and kblk_sz % QT == 0 and REL_MAX_DIST <= QT
    nq = s // QT
    nkb = s // kblk_sz
    sub = kblk_sz // QT
    chunk = min(KEY_CHUNK, s)
    assert kblk_sz % chunk == 0 and chunk % QT == 0
    r3 = lambda a: a.reshape(b, s, a.shape[-1])
    ikb = r3(pr["ikb"])
    iqt = pr["iq"].reshape(b, nq, QT, IDX_HEADS, LANE).transpose(0, 1, 4, 3, 2).reshape(b, nq, LANE, IDX_HEADS * QT)
    iwt = r3(pr["iw"])[..., :IDX_HEADS].transpose(0, 2, 1)
    aq, akb, avt = r3(pr["aq"]), r3(pr["akb"]), pr["avt"]
    padm = lambda a: jnp.pad(a, ((0, QT - N_META), (0, 0)))
    ikm, km, vtm = padm(me["ikb"]), padm(me["akb"]), padm(me["avb"]).T
    kk = jnp.arange(QT)
    ltri = (kk[None, :] < kk[:, None]).astype(BF)

    madd = pl.pallas_call(
        functools.partial(_select_prompt_kernel, n_top=float(n_top), seq=s, fill_kb=chunk), grid=(b, nq),
        in_specs=[pl.BlockSpec((1, 1, LANE, IDX_HEADS * QT), lambda bi, i: (bi, i, 0, 0)),
                  pl.BlockSpec((1, IDX_HEADS, QT), lambda bi, i: (bi, 0, i)),
                  pl.BlockSpec((1, s, LANE), lambda bi, i: (bi, 0, 0)),
                  pl.BlockSpec((QT, LANE), lambda bi, i: (0, 0)),
                  pl.BlockSpec((QT, QT), lambda bi, i: (0, 0))],
        out_specs=pl.BlockSpec((1, 1, s + QT, LANE), lambda bi, i: (bi, i, 0, 0)),
        out_shape=jax.ShapeDtypeStruct((b, nq, s + QT, LANE), F32),
        scratch_shapes=[pltpu.VMEM((nq + 1, QT, LANE), I32)],
        compiler_params=_cparams(("parallel", "arbitrary")), name="dsa_select_prompt",
    )(iqt, iwt, ikb, ikm, ltri)

    bb = _t5_bias_by_dist(rel_bias, 2 * QT)
    kj = jnp.arange(QT)[:, None]
    qi = jnp.arange(LANE)[None, :]
    bt = jnp.stack([bb[:, jnp.clip(qi - kj, 0, 2 * QT)],
                    bb[:, jnp.clip(QT + qi - kj, 0, 2 * QT)],
                    jnp.broadcast_to(bb[:, 2 * QT][:, None, None], (ATT_HEADS, QT, LANE)),
                    bb[:, jnp.clip(N_META + qi - kj, 0, 2 * QT)]])

    kblk = lambda i, kb: jnp.minimum(kb, i // sub)
    qmap = lambda bi, i, kb: (bi, i, 0)
    const2 = lambda bi, i, kb: (0, 0)
    return pl.pallas_call(
        functools.partial(_attn_prompt_kernel, sub=sub, pc=chunk // QT), grid=(b, nq, nkb),
        in_specs=[pl.BlockSpec((1, QT, 1024), qmap),
                  pl.BlockSpec((1, kblk_sz, 1024), lambda bi, i, kb: (bi, kblk(i, kb), 0)),
                  pl.BlockSpec((1, 1024, kblk_sz), lambda bi, i, kb: (bi, 0, kblk(i, kb))),
                  pl.BlockSpec((1, 1, kblk_sz, LANE), lambda bi, i, kb: (bi, i, kblk(i, kb), 0)),
                  pl.BlockSpec((QT, 1024), const2), pl.BlockSpec((1024, QT), const2),
                  pl.BlockSpec((1, 1, QT, LANE), lambda bi, i, kb: (bi, i, s // QT, 0)),
                  pl.BlockSpec(bt.shape, lambda bi, i, kb: (0, 0, 0, 0))],
        out_specs=pl.BlockSpec((1, QT, 1024), qmap),
        out_shape=jax.ShapeDtypeStruct((b, s, 1024), BF),
        scratch_shapes=[pltpu.VMEM((ATT_HEADS, LANE), F32), pltpu.VMEM((ATT_HEADS, LANE), F32),
                        pltpu.VMEM((1024, QT), F32)],
        compiler_params=_cparams(("parallel", "parallel", "arbitrary")), name="dsa_attn_prompt",
    )(aq, akb, avt, madd, km, vtm, madd, bt)


def _sample_scores(iq, iw, ikt):
    r = lax.dot_general(iq, ikt, _NT, preferred_element_type=F32)
    r = jnp.maximum(r, 0.0) * iw
    nq = r.shape[0] // IDX_HEADS
    s = r[0:nq]
    for h in range(1, IDX_HEADS):
        s = s + r[h * nq:(h + 1) * nq]
    return s + 0.0


def _select_sample_kernel(pt_ref, iq_ref, iw_ref, ikn_ref, utri_ref, *rest, n_top, n_pages, s_new):
    pages = rest[:PG_THR]
    madd_ref, keys_ref = rest[PG_THR], rest[PG_THR + 1]
    g = pl.program_id(1)
    iq = iq_ref[0]
    iw = iw_ref[0]
    nq = iq.shape[0] // IDX_HEADS
    nslots = n_pages + 1
    for u in range(PG_THR):
        keys_ref[g * PG_THR + u] = _sort_key(_sample_scores(iq, iw, pages[u][0, 0].astype(BF)))

    @pl.when(g == pl.num_programs(1) - 1)
    def _():
        col = lax.broadcasted_iota(I32, (nq, LANE), 1)
        row = lax.broadcasted_iota(I32, (nq, LANE), 0)
        k = _sort_key(_sample_scores(iq, iw, ikn_ref[0]))
        keys_ref[n_pages] = jnp.where(jnp.logical_and(col <= row, col < s_new), k, INT_MIN)
        t = _kth_largest_key(keys_ref, nslots, n_top, nq)

        def count(pred):
            def add(j, acc):
                return acc + jnp.where(pred(keys_ref[j]), 1.0, 0.0)
            return jnp.sum(lax.fori_loop(0, nslots, add, jnp.zeros((nq, LANE), F32)), axis=1, keepdims=True)

        def store(slot, madd):
            madd_ref[0, :, pl.ds(pl.multiple_of(slot * LANE, LANE), LANE)] = madd

        has_ties = jnp.max(count(lambda kk: kk >= t)) > n_top

        @pl.when(jnp.logical_not(has_ties))
        def _():
            def emit(slot, carry):
                store(slot, jnp.where(keys_ref[slot] >= t, 0.0, NEG))
                return carry
            lax.fori_loop(0, nslots, emit, 0)

        @pl.when(has_ties)
        def _():
            need = n_top - count(lambda kk: kk > t)

            def emit(slot, seen):
                kk = keys_ref[slot]
                eqf = jnp.where(kk == t, 1.0, 0.0)
                rank = seen + jnp.dot(eqf.astype(BF), utri_ref[...], preferred_element_type=F32)
                tie_ok = jnp.where(jnp.logical_and(kk == t, rank < need), 0.0, NEG)
                store(slot, jnp.where(kk > t, 0.0, tie_ok))
                return seen + jnp.sum(eqf, axis=1, keepdims=True)
            lax.fori_loop(0, nslots, emit, jnp.zeros((nq, 1), F32))


def _attn_sample_kernel(pt_ref, q_ref, madd_ref, maddn_ref, kn_ref, vn_ref, bts_ref, btn_ref, exp_ref, *rest):
    kp = rest[0:PG_ATT]
    vp = rest[PG_ATT:2 * PG_ATT]
    o_ref, m_ref, l_ref, acc_ref = rest[2 * PG_ATT:]
    g = pl.program_id(1)
    last = pl.num_programs(1) - 1
    scale = ATT_DH ** -0.5
    flat = PAGE_SIZE * ATT_HEADS

    @pl.when(g == 0)
    def _():
        m_ref[...] = jnp.full(m_ref.shape, NEG, F32)
        l_ref[...] = jnp.zeros(l_ref.shape, F32)
        acc_ref[...] = jnp.zeros(acc_ref.shape, F32)

    q = q_ref[0]

    def update(lg, vt):
        m_old = m_ref[...]
        m_new = jnp.maximum(m_old, jnp.max(lg, axis=1, keepdims=True))
        alpha = jnp.exp(m_old - m_new)
        p = jnp.exp(lg - m_new)
        l_ref[...] = alpha * l_ref[...] + jnp.sum(p, axis=1, keepdims=True)
        acc_ref[...] = alpha * acc_ref[...] + jnp.dot(p.astype(BF), vt, preferred_element_type=F32)
        m_ref[...] = m_new

    def spread(madd, ncols):
        sel = jnp.where(madd >= 0.0, 1.0, 0.0).astype(BF)
        rep = jnp.dot(sel, exp_ref[:, 0:ncols], preferred_element_type=F32)
        return jnp.concatenate([jnp.where(rep > 0.5, 0.0, NEG)] * ATT_HEADS, axis=0)

    ks = jnp.concatenate([kp[u][0, 0].reshape(flat, ATT_DH).astype(BF) for u in range(PG_ATT)], axis=0)
    vs = jnp.concatenate([vp[u][0, 0].reshape(flat, ATT_DH).astype(BF) for u in range(PG_ATT)], axis=0)
    madd = jnp.concatenate([spread(madd_ref[0, :, u * PAGE_SIZE:(u + 1) * PAGE_SIZE], flat)
                            for u in range(PG_ATT)], axis=1)
    bias = jnp.concatenate([bts_ref[0]] * (PG_ATT - 1) + [bts_ref[jnp.where(g == last, 1, 0)]], axis=1)
    update(lax.dot_general(q, ks, _NT, preferred_element_type=F32) * scale + bias + madd, vs)

    @pl.when(g == last)
    def _():
        lg = lax.dot_general(q, kn_ref[0], _NT, preferred_element_type=F32) * scale
        update(lg + btn_ref[...] + spread(maddn_ref[0], LANE), vn_ref[0])
        o_ref[0] = acc_ref[...] / l_ref[...]


def _dsa_sample(sm, cache_k, cache_v, cache_kidx, page_table, rel_bias, db, sd):
    n_pages = page_table.shape[1]
    past = n_pages * PAGE_SIZE
    n_top = min(TOPK_MAX, (past + sd) // 4)
    assert n_pages % PG_THR == 0 and n_pages % PG_ATT == 0 and sd <= NEW_PAD and sd == 8
    rows = IDX_HEADS * sd
    iq = sm["iq"].reshape(db, sd, IDX_HEADS, LANE)[..., :IDX_DIM]
    iq = iq.transpose(0, 2, 1, 3).reshape(db, rows, IDX_DIM)
    iw = sm["iw"].reshape(db, sd, LANE)[..., :IDX_HEADS].transpose(0, 2, 1).reshape(db, rows, 1)
    ikn = jnp.pad(sm["ikb"].reshape(db, sd, LANE)[..., :IDX_DIM], ((0, 0), (0, LANE - sd), (0, 0)))
    pt = page_table.reshape(-1).astype(I32)

    def page_spec(minor, per_step, u):
        return pl.BlockSpec((1, 1, PAGE_SIZE) + minor,
                            lambda bi, g, pt_ref: (0, pt_ref[bi * n_pages + g * per_step + u]) + (0,) * (1 + len(minor)))

    kv_minor = (ATT_HEADS, ATT_DH)

    seq3 = lambda a: pl.BlockSpec((1,) + a.shape[1:], lambda bi, g, pt_ref: (bi, 0, 0))
    full = lambda a: pl.BlockSpec(a.shape, lambda bi, g, pt_ref: (0,) * a.ndim)

    kk = jnp.arange(LANE)
    utri = (kk[:, None] < kk[None, :]).astype(BF)
    madd = pl.pallas_call(
        functools.partial(_select_sample_kernel, n_top=float(n_top), n_pages=n_pages, s_new=sd),
        grid_spec=pltpu.PrefetchScalarGridSpec(
            num_scalar_prefetch=1, grid=(db, n_pages // PG_THR),
            in_specs=[seq3(iq), seq3(iw), seq3(ikn), full(utri)]
                     + [page_spec((IDX_DIM,), PG_THR, u) for u in range(PG_THR)],
            out_specs=pl.BlockSpec((1, sd, past + LANE), lambda bi, g, pt_ref: (bi, 0, 0)),
            scratch_shapes=[pltpu.VMEM((n_pages + 1, sd, LANE), I32)]),
        out_shape=jax.ShapeDtypeStruct((db, sd, past + LANE), F32),
        compiler_params=_cparams(("parallel", "arbitrary")), name="dsa_select_sample",
    )(pt, iq, iw, ikn, utri, *([cache_kidx] * PG_THR))

    bb = _t5_bias_by_dist(rel_bias, 2 * PAGE_SIZE)
    qq = jnp.arange(sd)[:, None]
    tt = jnp.arange(PAGE_SIZE)[None, :]
    same_head = jnp.eye(ATT_HEADS, dtype=bool)[:, None, None, :]

    def head_lanes(tab):
        t = tab.shape[-1]
        return jnp.where(same_head, tab[..., None], NEG).reshape(ATT_HEADS * sd, t * ATT_HEADS)

    bts = jnp.stack([head_lanes(jnp.broadcast_to(bb[:, 2 * PAGE_SIZE][:, None, None], (ATT_HEADS, sd, PAGE_SIZE))),
                     head_lanes(bb[:, jnp.clip(PAGE_SIZE + qq - tt, 0, 2 * PAGE_SIZE)])])
    btn = head_lanes(bb[:, jnp.clip(qq - jnp.arange(NEW_PAD)[None, :], 0, 2 * PAGE_SIZE)])
    lane_tok = jnp.arange(PAGE_SIZE * ATT_HEADS) // ATT_HEADS
    expand = (jnp.arange(PAGE_SIZE)[:, None] == lane_tok[None, :]).astype(BF)
    q3 = sm["aq"].reshape(db, sd, ATT_HEADS, ATT_DH).transpose(0, 2, 1, 3).reshape(db, rows, ATT_DH)
    flatn = lambda a: jnp.pad(a.reshape(db, sd * ATT_HEADS, ATT_DH), ((0, 0), (0, (NEW_PAD - sd) * ATT_HEADS), (0, 0)))
    kn, vn = flatn(sm["akb"]), flatn(sm["avb"])

    out = pl.pallas_call(
        _attn_sample_kernel,
        grid_spec=pltpu.PrefetchScalarGridSpec(
            num_scalar_prefetch=1, grid=(db, n_pages // PG_ATT),
            in_specs=[seq3(q3),
                      pl.BlockSpec((1, sd, PG_ATT * PAGE_SIZE), lambda bi, g, pt_ref: (bi, 0, g)),
                      pl.BlockSpec((1, sd, LANE), lambda bi, g, pt_ref: (bi, 0, n_pages)),
                      seq3(kn), seq3(vn), full(bts), full(btn), full(expand)]
                     + [page_spec(kv_minor, PG_ATT, u) for u in range(PG_ATT)]
                     + [page_spec(kv_minor, PG_ATT, u) for u in range(PG_ATT)],
            out_specs=pl.BlockSpec((1, rows, ATT_DH), lambda bi, g, pt_ref: (bi, 0, 0)),
            scratch_shapes=[pltpu.VMEM((rows, 1), F32), pltpu.VMEM((rows, 1), F32),
                            pltpu.VMEM((rows, ATT_DH), F32)]),
        out_shape=jax.ShapeDtypeStruct((db, rows, ATT_DH), F32),
        compiler_params=_cparams(("parallel", "arbitrary")), name="dsa_attn_sample",
    )(pt, q3, madd, madd, kn, vn, bts, btn, expand, *([cache_k] * PG_ATT), *([cache_v] * PG_ATT))
    return out.reshape(db, ATT_HEADS, sd, ATT_DH).transpose(0, 2, 1, 3).reshape(db, sd, ATT_HEADS * ATT_DH)


def _merge_kernel(x_ref, ro_ref, rg_ref, ao_ref, ga_ref, gb_ref, rng_ref, wr_ref, wa_ref, wo_ref, y_ref, yr_ref):
    for h in range(RET_HEADS):
        sl = slice(h * RET_DV, (h + 1) * RET_DV)
        rg = rg_ref[:, sl]
        yr_ref[:, sl] = (rg * jax.nn.sigmoid(rg) * _rms(ro_ref[:, sl], rng_ref[:, sl])).astype(BF)
    y_r = jnp.dot(yr_ref[...], wr_ref[...], preferred_element_type=F32)
    y_a = jnp.dot(ao_ref[...].astype(BF), wa_ref[...], preferred_element_type=F32)
    merged = jax.nn.sigmoid(ga_ref[...]) * y_r + jax.nn.sigmoid(gb_ref[...]) * y_a
    y_ref[...] = x_ref[...] + jnp.dot(merged.astype(BF), wo_ref[...], preferred_element_type=F32)


def _merge(x2d, ret_o, rg, att_o, ga, gb, ret_norm_g, wr, wa, wo):
    n = x2d.shape[0]
    t = min(TOK_TILE, n)
    assert n % t == 0
    row = pl.BlockSpec((t, 1024), lambda i: (i, 0))
    full = lambda a: pl.BlockSpec(a.shape, lambda i: (0,) * a.ndim)
    rng = ret_norm_g.reshape(1, RET_HEADS * RET_DV)
    return pl.pallas_call(
        _merge_kernel, grid=(n // t,),
        in_specs=[row] * 6 + [full(rng), full(wr), full(wa), full(wo)],
        out_specs=row, out_shape=jax.ShapeDtypeStruct((n, 1024), F32),
        scratch_shapes=[pltpu.VMEM((t, 1024), BF)],
        compiler_params=_cparams(("parallel",)), name="merge",
    )(x2d, ret_o, rg, att_o, ga, gb, rng, wr, wa, wo)


def _peer_route_kernel(x_ref, g_ref, wq_ref, sk1_ref, sk2_ref, hn_ref, s1_ref, s2_ref, st_ref,
                       qt_ref, v1_ref, v2_ref, c_ref, sc_ref):
    hn = _rms(x_ref[...], g_ref[...]).astype(BF)
    hn_ref[...] = hn
    qt_ref[...] = lax.dot_general(wq_ref[...], hn, _NT, preferred_element_type=F32)
    half = PEER_DQ // 2
    ninf = -jnp.inf

    def top_vals(src_ref, h, dst_ref):
        x = src_ref[h]
        for r in range(PEER_TOPK):
            v = jnp.max(x, axis=0, keepdims=True)
            dst_ref[r:r + 1, :] = v
            x = jnp.where(x == v, ninf, x)

    def head(h, carry):
        o = pl.multiple_of(h * PEER_DQ, PEER_DQ)
        q1 = qt_ref[pl.ds(o, half), :].astype(BF)
        q2 = qt_ref[pl.ds(o + half, half), :].astype(BF)
        s1_ref[h] = jnp.dot(sk1_ref[h], q1, preferred_element_type=F32)
        s2_ref[h] = jnp.dot(sk2_ref[h], q2, preferred_element_type=F32)
        top_vals(s1_ref, h, v1_ref)
        top_vals(s2_ref, h, v2_ref)
        v2 = v2_ref[...]
        for a in range(PEER_TOPK):
            c_ref[a * PEER_TOPK:(a + 1) * PEER_TOPK, :] = v1_ref[a:a + 1, :] + v2
        for r in range(PEER_TOPK):
            c = c_ref[...]
            v = jnp.max(c, axis=0, keepdims=True)
            sc_ref[r:r + 1, :] = v
            c_ref[...] = jnp.where(c == v, ninf, c)
        sc = sc_ref[...]
        z = jnp.sum(jnp.exp(sc - sc[0:1, :]), axis=0, keepdims=True)
        st_ref[h, 0:1, :] = sc[PEER_TOPK - 1:PEER_TOPK, :]
        st_ref[h, 1:2, :] = v1_ref[0:1, :]
        st_ref[h, 2:3, :] = v2_ref[0:1, :]
        st_ref[h, 3:4, :] = 1.0 / z
        st_ref[h, 4:8, :] = jnp.zeros((4, z.shape[1]), F32)
        return carry

    lax.fori_loop(0, PEER_HEADS, head, 0)


def _peer_expert_kernel(hn_ref, x_ref, s1_ref, s2_ref, st_ref, u_ref, vt_ref, y_ref,
                        e1_ref, e2_ref, act_ref, w_ref, out_ref):
    c = pl.program_id(1)
    slabs = EXP_CHUNK // PEER_KEYS

    @pl.when(c == 0)
    def _():
        for h in range(PEER_HEADS):
            st = st_ref[h]
            e1_ref[h] = jnp.exp(s1_ref[h] - st[1:2, :]) * st[3:4, :]
            e2_ref[h] = jnp.exp(s2_ref[h] - st[2:3, :])
        out_ref[...] = jnp.zeros(out_ref.shape, F32)

    act_ref[...] = lax.dot_general(u_ref[...], hn_ref[...], _NT, preferred_element_type=F32)

    def slab(al, carry):
        a = c * slabs + al
        g = None
        for h in range(PEER_HEADS):
            ssum = s1_ref[h, pl.ds(a, 1), :] + s2_ref[h]
            gh = jnp.where(ssum >= st_ref[h, 0:1, :], e2_ref[h], 0.0) * e1_ref[h, pl.ds(a, 1), :]
            g = gh if g is None else g + gh
        o = pl.multiple_of(al * PEER_KEYS, PEER_KEYS)
        x = act_ref[pl.ds(o, PEER_KEYS), :]
        gelu = 0.5 * x * (1.0 + lax.erf(x * math.sqrt(0.5)))
        w_ref[pl.ds(o, PEER_KEYS), :] = (g * gelu).astype(BF)
        return carry

    lax.fori_loop(0, slabs, slab, 0)
    out_ref[...] += jnp.dot(vt_ref[...], w_ref[...], preferred_element_type=F32)

    @pl.when(c == pl.num_programs(1) - 1)
    def _():
        y_ref[...] = x_ref[...] + out_ref[...].T


def _peer(x2d, norm_g, wq_t, sk1, sk2, u_bf, vt_bf):
    n = x2d.shape[0]
    t = min(TOK_TILE, n)
    assert n % t == 0 and t % LANE == 0
    nt = n // t
    n_exp = u_bf.shape[0]
    assert n_exp == PEER_KEYS * PEER_KEYS and n_exp % EXP_CHUNK == 0
    g = norm_g.reshape(1, D_MODEL)
    row = lambda width: pl.BlockSpec((t, width), lambda i: (i, 0))
    full = lambda a: pl.BlockSpec(a.shape, lambda i: (0,) * a.ndim)
    tk = pl.BlockSpec((PEER_HEADS, PEER_KEYS, t), lambda i: (0, 0, i))
    hn, s1, s2, st = pl.pallas_call(
        _peer_route_kernel, grid=(nt,),
        in_specs=[row(D_MODEL), full(g), full(wq_t), full(sk1), full(sk2)],
        out_specs=[row(D_MODEL), tk, tk, pl.BlockSpec((PEER_HEADS, 8, t), lambda i: (0, 0, i))],
        out_shape=[jax.ShapeDtypeStruct((n, D_MODEL), BF),
                   jax.ShapeDtypeStruct((PEER_HEADS, PEER_KEYS, n), F32),
                   jax.ShapeDtypeStruct((PEER_HEADS, PEER_KEYS, n), F32),
                   jax.ShapeDtypeStruct((PEER_HEADS, 8, n), F32)],
        scratch_shapes=[pltpu.VMEM((PEER_HEADS * PEER_DQ, t), F32),
                        pltpu.VMEM((PEER_TOPK, t), F32), pltpu.VMEM((PEER_TOPK, t), F32),
                        pltpu.VMEM((PEER_TOPK * PEER_TOPK, t), F32), pltpu.VMEM((PEER_TOPK, t), F32)],
        compiler_params=_cparams(("parallel",)), name="peer_route",
    )(x2d, g, wq_t, sk1, sk2)

    tk2 = pl.BlockSpec((PEER_HEADS, PEER_KEYS, t), lambda i, c: (0, 0, i))
    row2 = lambda width: pl.BlockSpec((t, width), lambda i, c: (i, 0))
    return pl.pallas_call(
        _peer_expert_kernel, grid=(nt, n_exp // EXP_CHUNK),
        in_specs=[row2(D_MODEL), row2(D_MODEL), tk2, tk2,
                  pl.BlockSpec((PEER_HEADS, 8, t), lambda i, c: (0, 0, i)),
                  pl.BlockSpec((EXP_CHUNK, D_MODEL), lambda i, c: (c, 0)),
                  pl.BlockSpec((D_MODEL, EXP_CHUNK), lambda i, c: (0, c))],
        out_specs=row2(D_MODEL), out_shape=jax.ShapeDtypeStruct((n, D_MODEL), F32),
        scratch_shapes=[pltpu.VMEM((PEER_HEADS, PEER_KEYS, t), F32), pltpu.VMEM((PEER_HEADS, PEER_KEYS, t), F32),
                        pltpu.VMEM((EXP_CHUNK, t), F32), pltpu.VMEM((EXP_CHUNK, t), BF),
                        pltpu.VMEM((D_MODEL, t), F32)],
        compiler_params=_cparams(("parallel", "arbitrary")), name="peer_expert",
    )(hn, x2d, s1, s2, st, u_bf, vt_bf)


def _rope_tables(pos):
    half = RET_DK // 2
    inv = ROPE_BASE ** (-jnp.arange(half, dtype=F32) / half)
    ang = pos.astype(F32)[:, None] * inv[None, :]
    cos, sin = jnp.cos(ang), jnp.sin(ang)
    return jnp.concatenate([cos, cos], axis=1), jnp.concatenate([-sin, sin], axis=1)


def _split_w_in(w):
    sizes = (RET_HEADS * RET_DK, RET_HEADS * RET_DK, RET_HEADS * RET_DV, RET_HEADS * RET_DV,
             ATT_HEADS * ATT_DH, ATT_HEADS * ATT_DH, ATT_HEADS * ATT_DH,
             IDX_HEADS * IDX_DIM, IDX_DIM, IDX_HEADS, D_MODEL, D_MODEL)
    assert w.shape[1] == sum(sizes)
    parts, o = [], 0
    for s in sizes:
        parts.append(w[:, o:o + s])
        o += s
    rq, rk, rv, rg, aq, ak, av, iq, ik, iw, ga, gb = parts
    d = w.shape[0]
    iq = jnp.pad(iq.reshape(d, IDX_HEADS, IDX_DIM), ((0, 0), (0, 0), (0, LANE - IDX_DIM))).reshape(d, -1)
    ik = jnp.pad(ik, ((0, 0), (0, LANE - IDX_DIM)))
    iw = jnp.pad(iw, ((0, 0), (0, LANE - IDX_HEADS)))
    cat = lambda xs: jnp.concatenate(xs, axis=1).astype(BF)
    return cat([rq, rk, rv, rg]), cat([aq, ak, av]), cat([iq, ik, iw, ga, gb])


def kernel(x_prompt, x_sample, cache_k, cache_v, cache_kidx, state_ret, page_table, meta_tokens, rel_bias,
           attn_norm_g, w_in, q_norm_g, k_norm_g, ret_norm_g, w_ret_o, w_att_o, w_out, ffn_norm_g,
           peer_w_q, peer_sub_keys, peer_u, peer_v):
    b, s, d = x_prompt.shape
    db, sd, _ = x_sample.shape
    assert w_in.shape[0] == 1 and d == D_MODEL and s % RET_CHUNK == 0
    n_pages = page_table.shape[1]
    past = n_pages * PAGE_SIZE
    n_top_p = min(TOPK_MAX, s // 4)
    l = 0

    wts = _split_w_in(w_in[l])
    qg = q_norm_g[l].reshape(1, ATT_DH)
    kg = k_norm_g[l].reshape(1, ATT_DH)
    wr, wa, wo = w_ret_o[l].astype(BF), w_att_o[l].astype(BF), w_out[l].astype(BF)
    wq_t = peer_w_q[l].T.astype(BF)
    sk1 = peer_sub_keys[l][:, 0].astype(BF)
    sk2 = peer_sub_keys[l][:, 1].astype(BF)
    u_bf = peer_u[l].astype(BF)
    vt_bf = peer_v[l].T.astype(BF)
    hkv = (ATT_HEADS, ATT_DH)

    xr = x_prompt.reshape(b * s, d)
    pr = _in_proj(xr, attn_norm_g[l], wts, *_rope_tables(N_META + jnp.arange(s)), qg, kg, vt_seq_len=s)
    me = _in_proj(meta_tokens.astype(x_prompt.dtype), attn_norm_g[l], wts, *_rope_tables(jnp.arange(N_META)),
                  qg, kg)
    m3 = lambda a: a.reshape(1, N_META, a.shape[-1])
    _, s_meta = _retention(m3(me["rq"]), m3(me["rk"]), m3(me["rv"]),
                           jnp.zeros((1, RET_HEADS, RET_DK, RET_DV), F32), N_META)
    r3 = lambda a: a.reshape(b, s, a.shape[-1])
    ret_o, s_fin = _retention(r3(pr["rq"]), r3(pr["rk"]), r3(pr["rv"]), s_meta, RET_CHUNK)
    att_o = _dsa_prompt(pr, me, rel_bias, b, s, n_top_p)
    x1 = _merge(xr, ret_o.reshape(b * s, -1), pr["rg"], att_o.reshape(b * s, -1), pr["ga"], pr["gb"],
                ret_norm_g[l], wr, wa, wo)
    y_prompt = _peer(x1, ffn_norm_g[l], wq_t, sk1, sk2, u_bf, vt_bf).reshape(b, s, d)

    def with_meta(real, meta):
        meta = jnp.broadcast_to(meta[None], (b,) + meta.shape)
        return jnp.concatenate([meta, real.reshape(b, s, -1)], axis=1)

    k_prompt = with_meta(pr["ak"], me["ak"]).reshape(1, b, s + N_META, *hkv).astype(cache_k.dtype)
    v_prompt = with_meta(pr["av"], me["av"]).reshape(1, b, s + N_META, *hkv).astype(cache_v.dtype)
    kidx_prompt = with_meta(pr["ik"], me["ik"])[None].astype(cache_kidx.dtype)
    ret_state_prompt = s_fin[None].astype(state_ret.dtype)

    xs = x_sample.reshape(db * sd, d)
    pos_s = jnp.tile(past + jnp.arange(sd), (db * sd) // sd)
    sm = _in_proj(xs, attn_norm_g[l], wts, *_rope_tables(pos_s), qg, kg)
    s3 = lambda a: a.reshape(db, sd, a.shape[-1])
    ret_os, s_new = _retention(s3(sm["rq"]), s3(sm["rk"]), s3(sm["rv"]), state_ret[l].astype(F32), sd)
    att_os = _dsa_sample(sm, cache_k, cache_v, cache_kidx, page_table, rel_bias, db, sd)
    x1s = _merge(xs, ret_os.reshape(db * sd, -1), sm["rg"], att_os.reshape(db * sd, -1), sm["ga"], sm["gb"],
                 ret_norm_g[l], wr, wa, wo)
    y_sample = _peer(x1s, ffn_norm_g[l], wq_t, sk1, sk2, u_bf, vt_bf).reshape(db, sd, d)

    k_sample = sm["ak"].reshape(1, db, sd, *hkv).astype(cache_k.dtype)
    v_sample = sm["av"].reshape(1, db, sd, *hkv).astype(cache_v.dtype)
    kidx_sample = sm["ik"].reshape(1, db, sd, IDX_DIM).astype(cache_kidx.dtype)
    ret_state_sample = s_new[None].astype(state_ret.dtype)

    return (y_prompt, y_sample, k_prompt, v_prompt, kidx_prompt, ret_state_prompt,
            k_sample, v_sample, kidx_sample, ret_state_sample)
```

```python
import functools
import math

import jax
import jax.numpy as jnp
from jax import lax
from jax.experimental import pallas as pl
from jax.experimental.pallas import tpu as pltpu

F32 = jnp.float32
BF = jnp.bfloat16
I32 = jnp.int32

D_MODEL = 1024
PAGE_SIZE = 128
N_META = 16
RET_HEADS = 4
RET_DK = 128
RET_DV = 256
RET_CHUNK = 128
ROPE_BASE = 10000.0
ATT_HEADS = 8
ATT_DH = 128
IDX_HEADS = 8
IDX_DIM = 64
TOPK_MAX = 256
REL_BUCKETS = 32
REL_MAX_DIST = 128
PEER_HEADS = 8
PEER_KEYS = 128
PEER_DQ = 128
PEER_TOPK = 16
EPS = 1e-6
NEG = -1e30

LANE = 128
TOK_TILE = 512
QT = 128
KBLK_MAX = 1024
KEY_CHUNK = 512
EXP_CHUNK = 1024
PG_THR = 16
PG_ATT = 8
NEW_PAD = 16
VMEM_LIMIT = 52 * 1024 * 1024
INT_MIN = -2 ** 31
KEY_NEG_INF = -2139095041

_NT = (((1,), (1,)), ((), ()))
_TN = (((0,), (0,)), ((), ()))


def _cparams(sem):
    return pltpu.CompilerParams(dimension_semantics=sem, vmem_limit_bytes=VMEM_LIMIT)


def _rms(x, g):
    return x * lax.rsqrt(jnp.mean(x * x, axis=-1, keepdims=True) + EPS) * g


def _sort_key(s):
    b = lax.bitcast_convert_type(s, I32)
    return b ^ ((b >> 31) & 0x7FFFFFFF)


def _proj_ret_kernel(x_ref, g_ref, w_ref, cos_ref, sin_ref, rq_ref, rk_ref, rv_ref, rg_ref):
    h = _rms(x_ref[...], g_ref[...]).astype(BF)
    cosf = cos_ref[...]
    sinf = sin_ref[...]
    nqk = RET_HEADS * RET_DK
    for off, ref, sc in ((0, rq_ref, None), (nqk, rk_ref, RET_DK ** -0.5)):
        z = jnp.dot(h, w_ref[:, off:off + nqk], preferred_element_type=F32)
        for hd in range(RET_HEADS):
            sl = slice(hd * RET_DK, (hd + 1) * RET_DK)
            xh = z[:, sl]
            r = xh * cosf + pltpu.roll(xh, RET_DK // 2, 1) * sinf
            ref[:, sl] = r if sc is None else r * sc
    nv = RET_HEADS * RET_DV
    rv_ref[...] = jnp.dot(h, w_ref[:, 2 * nqk:2 * nqk + nv], preferred_element_type=F32)
    rg_ref[...] = jnp.dot(h, w_ref[:, 2 * nqk + nv:2 * nqk + 2 * nv], preferred_element_type=F32)


def _proj_att_kernel(x_ref, g_ref, w_ref, qg_ref, kg_ref, aq_ref, ak_ref, av_ref, akb_ref, avb_ref, *maybe_avt_ref):
    h = _rms(x_ref[...], g_ref[...]).astype(BF)
    n = ATT_HEADS * ATT_DH
    q = jnp.dot(h, w_ref[:, 0:n], preferred_element_type=F32)
    k = jnp.dot(h, w_ref[:, n:2 * n], preferred_element_type=F32)
    for hd in range(ATT_HEADS):
        sl = slice(hd * ATT_DH, (hd + 1) * ATT_DH)
        aq_ref[:, sl] = _rms(q[:, sl], qg_ref[...]).astype(BF)
        kn = _rms(k[:, sl], kg_ref[...])
        ak_ref[:, sl] = kn
        akb_ref[:, sl] = kn.astype(BF)
    v = jnp.dot(h, w_ref[:, 2 * n:3 * n], preferred_element_type=F32)
    av_ref[...] = v
    avb_ref[...] = v.astype(BF)
    for avt_ref in maybe_avt_ref:
        avt_ref[0] = v.T.astype(BF)


def _proj_idx_kernel(x_ref, g_ref, w_ref, iq_ref, ik_ref, ikb_ref, iw_ref, ga_ref, gb_ref):
    h = _rms(x_ref[...], g_ref[...]).astype(BF)
    nq = IDX_HEADS * LANE
    iq = jnp.dot(h, w_ref[:, 0:nq], preferred_element_type=F32)
    iq_ref[...] = (iq * IDX_DIM ** -0.5).astype(BF)
    ik = jnp.dot(h, w_ref[:, nq:nq + LANE], preferred_element_type=F32)
    ik_ref[...] = ik[:, 0:IDX_DIM]
    ikb_ref[...] = ik.astype(BF)
    iw = jnp.dot(h, w_ref[:, nq + LANE:nq + 2 * LANE], preferred_element_type=F32)
    iw_ref[...] = iw * IDX_HEADS ** -0.5
    o = nq + 2 * LANE
    ga_ref[...] = jnp.dot(h, w_ref[:, o:o + D_MODEL], preferred_element_type=F32)
    gb_ref[...] = jnp.dot(h, w_ref[:, o + D_MODEL:o + 2 * D_MODEL], preferred_element_type=F32)


def _in_proj(x2d, norm_g, wts, cosf, sinf, qg, kg, vt_seq_len=None):
    n = x2d.shape[0]
    t = min(TOK_TILE, n)
    assert n % t == 0 and cosf.shape[0] % t == 0
    nt = n // t
    ptiles = cosf.shape[0] // t
    w1, w2, w3 = wts
    row = lambda width: pl.BlockSpec((t, width), lambda i: (i, 0))
    full = lambda a: pl.BlockSpec(a.shape, lambda i: (0,) * a.ndim)
    tab = pl.BlockSpec((t, LANE), lambda i: (i % ptiles, 0))
    sds = lambda width, dt: jax.ShapeDtypeStruct((n, width), dt)
    g = norm_g.reshape(1, D_MODEL)
    cp = _cparams(("parallel",))

    rq, rk, rv, rg = pl.pallas_call(
        _proj_ret_kernel, grid=(nt,),
        in_specs=[row(D_MODEL), full(g), full(w1), tab, tab],
        out_specs=[row(512), row(512), row(1024), row(1024)],
        out_shape=[sds(512, F32), sds(512, F32), sds(1024, F32), sds(1024, F32)],
        compiler_params=cp, name="proj_ret")(x2d, g, w1, cosf, sinf)

    att_specs = [row(1024)] * 5
    att_shapes = [sds(1024, BF), sds(1024, F32), sds(1024, F32), sds(1024, BF), sds(1024, BF)]
    if vt_seq_len is not None:
        assert vt_seq_len % t == 0 and n % vt_seq_len == 0 and t % LANE == 0
        tps = vt_seq_len // t
        att_specs.append(pl.BlockSpec((1, 1024, t), lambda i: (i // tps, 0, i % tps)))
        att_shapes.append(jax.ShapeDtypeStruct((n // vt_seq_len, 1024, vt_seq_len), BF))
    att_out = pl.pallas_call(
        _proj_att_kernel, grid=(nt,),
        in_specs=[row(D_MODEL), full(g), full(w2), full(qg), full(kg)],
        out_specs=att_specs, out_shape=att_shapes,
        compiler_params=cp, name="proj_att")(x2d, g, w2, qg, kg)
    aq, ak, av, akb, avb = att_out[:5]
    avt = att_out[5] if vt_seq_len is not None else None

    iq, ik, ikb, iw, ga, gb = pl.pallas_call(
        _proj_idx_kernel, grid=(nt,),
        in_specs=[row(D_MODEL), full(g), full(w3)],
        out_specs=[row(1024), row(IDX_DIM), row(LANE), row(LANE), row(1024), row(1024)],
        out_shape=[sds(1024, BF), sds(IDX_DIM, F32), sds(LANE, BF), sds(LANE, F32),
                   sds(1024, F32), sds(1024, F32)],
        compiler_params=cp, name="proj_idx")(x2d, g, w3)

    return dict(rq=rq, rk=rk, rv=rv, rg=rg, aq=aq, ak=ak, av=av, akb=akb, avb=avb, avt=avt,
                iq=iq, ik=ik, ikb=ikb, iw=iw, ga=ga, gb=gb)


def _ret_kernel(q_ref, k_ref, v_ref, s0_ref, dmat_ref, dq_ref, dk_ref, gc_ref, o_ref, sout_ref, st_ref):
    c = pl.program_id(1)

    @pl.when(c == 0)
    def _():
        st_ref[...] = s0_ref[0]

    for h in range(RET_HEADS):
        qh = q_ref[0, :, h * RET_DK:(h + 1) * RET_DK].astype(BF)
        kh = k_ref[0, :, h * RET_DK:(h + 1) * RET_DK]
        vh = v_ref[0, :, h * RET_DV:(h + 1) * RET_DV].astype(BF)
        s_old = st_ref[h]
        sc = lax.dot_general(qh, kh.astype(BF), _NT, preferred_element_type=F32) * dmat_ref[h]
        intra = jnp.dot(sc.astype(BF), vh, preferred_element_type=F32)
        inter = jnp.dot(qh, s_old.astype(BF), preferred_element_type=F32) * dq_ref[h]
        o_ref[0, :, h * RET_DV:(h + 1) * RET_DV] = intra + inter
        kd = (kh * dk_ref[h]).astype(BF)
        upd = lax.dot_general(kd, vh, _TN, preferred_element_type=F32)
        st_ref[h] = s_old * gc_ref[h] + upd

    @pl.when(c == pl.num_programs(1) - 1)
    def _():
        sout_ref[0] = st_ref[...]


def _retention(q, k, v, s0, chunk):
    b, t, _ = q.shape
    assert t % chunk == 0
    nc = t // chunk
    ld = jnp.log1p(-jnp.exp2(-5.0 - jnp.arange(RET_HEADS, dtype=F32)))
    i = jnp.arange(chunk, dtype=F32)
    diff = i[:, None] - i[None, :]
    dmat = jnp.where(diff >= 0, jnp.exp(ld[:, None, None] * jnp.maximum(diff, 0.0)[None]), 0.0)
    dq = jnp.exp((i[None, :] + 1.0) * ld[:, None])[..., None]
    dk = jnp.exp((chunk - 1.0 - i)[None, :] * ld[:, None])[..., None]
    gc = jnp.exp(chunk * ld).reshape(RET_HEADS, 1, 1)
    shared = s0.shape[0] == 1
    full = lambda a: pl.BlockSpec(a.shape, lambda bi, ci: (0,) * a.ndim)
    seq = lambda width: pl.BlockSpec((1, chunk, width), lambda bi, ci: (bi, ci, 0))
    st_spec = pl.BlockSpec((1, RET_HEADS, RET_DK, RET_DV),
                           (lambda bi, ci: (0, 0, 0, 0)) if shared else (lambda bi, ci: (bi, 0, 0, 0)))
    return pl.pallas_call(
        _ret_kernel, grid=(b, nc),
        in_specs=[seq(RET_HEADS * RET_DK), seq(RET_HEADS * RET_DK), seq(RET_HEADS * RET_DV),
                  st_spec, full(dmat), full(dq), full(dk), full(gc)],
        out_specs=[seq(RET_HEADS * RET_DV),
                   pl.BlockSpec((1, RET_HEADS, RET_DK, RET_DV), lambda bi, ci: (bi, 0, 0, 0))],
        out_shape=[jax.ShapeDtypeStruct((b, t, RET_HEADS * RET_DV), F32),
                   jax.ShapeDtypeStruct((b, RET_HEADS, RET_DK, RET_DV), F32)],
        scratch_shapes=[pltpu.VMEM((RET_HEADS, RET_DK, RET_DV), F32)],
        compiler_params=_cparams(("parallel", "arbitrary")), name="retention",
    )(q, k, v, s0, dmat, dq, dk, gc)


def _t5_bias_by_dist(rel_bias, maxd):
    d = jnp.arange(maxd + 1, dtype=I32)
    max_exact = REL_BUCKETS // 2
    df = jnp.maximum(d, 1).astype(F32)
    large = max_exact + (jnp.log(df / max_exact) / math.log(REL_MAX_DIST / max_exact)
                         * (REL_BUCKETS - max_exact)).astype(I32)
    large = jnp.minimum(large, REL_BUCKETS - 1)
    bucket = jnp.where(d < max_exact, d, large)
    return rel_bias[bucket].T.astype(F32)


def _idx_scores_t(ikt, iqt, w):
    nq = iqt.shape[1] // IDX_HEADS
    r = jnp.dot(ikt, iqt, preferred_element_type=F32)
    s = None
    for h in range(IDX_HEADS):
        rh = jnp.maximum(r[:, h * nq:(h + 1) * nq], 0.0) * w[h:h + 1, :]
        s = rh if s is None else s + rh
    return s + 0.0


def _kth_largest_key(keys_ref, ntiles, n_top, rows):
    def step(sidx, t):
        bitv = jnp.left_shift(jnp.int32(1), 31 - sidx)
        cand = ((t ^ INT_MIN) | bitv) ^ INT_MIN
        cb = jnp.broadcast_to(cand, (rows, LANE))

        def count(j, acc):
            return acc + jnp.where(keys_ref[j] >= cb, 1.0, 0.0)

        acc = lax.fori_loop(0, ntiles, count, jnp.zeros((rows, LANE), F32))
        cnt = jnp.sum(acc, axis=1, keepdims=True)
        return jnp.where(cnt >= n_top, cand, t)

    t = lax.fori_loop(0, 32, step, jnp.full((rows, 1), INT_MIN, I32))
    return jnp.maximum(t, KEY_NEG_INF)


def _select_prompt_kernel(iq_ref, iwt_ref, ik_ref, ikm_ref, ltri_ref, madd_ref, keys_ref, *, n_top, seq, fill_kb):
    i = pl.program_id(1)
    nq = pl.num_programs(1)
    iqt = iq_ref[0, 0]
    w = iwt_ref[0]
    krow = lax.broadcasted_iota(I32, (QT, LANE), 0)
    qcol = lax.broadcasted_iota(I32, (QT, LANE), 1)
    big = 1e9

    keys_ref[0] = jnp.where(krow < N_META, _sort_key(_idx_scores_t(ikm_ref[...], iqt, w)), INT_MIN)
    per_blk = fill_kb // QT
    nblk = (i + per_blk) // per_blk

    def fill(blk, carry):
        s = _idx_scores_t(ik_ref[0, pl.ds(pl.multiple_of(blk * fill_kb, fill_kb), fill_kb), :], iqt, w)
        for u in range(per_blk):
            j = blk * per_blk + u
            visible = (krow + j * QT) <= (qcol + i * QT)
            keys_ref[j + 1] = jnp.where(visible, _sort_key(s[u * QT:(u + 1) * QT]), INT_MIN)
        return carry

    lax.fori_loop(0, nblk, fill, 0)
    nslots = nblk * per_blk + 1

    def count(pred):
        def add(blk, acc):
            for u in range(per_blk):
                acc = acc + jnp.where(pred(keys_ref[blk * per_blk + u + 1]), 1.0, 0.0)
            return acc
        acc = lax.fori_loop(0, nblk, add, jnp.where(pred(keys_ref[0]), 1.0, 0.0))
        return jnp.sum(acc, axis=0, keepdims=True)

    def cond(st):
        return jnp.logical_and(st[0] < 32, st[3] > 0)

    def step(st):
        sidx, t, cnt_t, _ = st
        bitv = jnp.left_shift(jnp.int32(1), 31 - sidx)
        cand = ((t ^ INT_MIN) | bitv) ^ INT_MIN
        cnt = count(lambda k: k >= cand)
        take = cnt >= n_top
        t = jnp.where(take, cand, t)
        cnt_t = jnp.where(take, cnt, cnt_t)
        return sidx + 1, t, cnt_t, (jnp.max(cnt_t) > n_top).astype(I32)

    _, t, cnt_t, _ = lax.while_loop(
        cond, step, (jnp.int32(0), jnp.full((1, LANE), INT_MIN, I32), jnp.full((1, LANE), big, F32), jnp.int32(1)))
    t = jnp.maximum(t, KEY_NEG_INF)
    has_ties = jnp.max(jnp.where(cnt_t < 0.5 * big, cnt_t, 0.0)) > n_top

    def store(slot, madd):
        @pl.when(slot == 0)
        def _():
            madd_ref[0, 0, seq:seq + QT, :] = madd

        @pl.when(slot > 0)
        def _():
            madd_ref[0, 0, pl.ds(pl.multiple_of((slot - 1) * QT, QT), QT), :] = madd

    @pl.when(jnp.logical_not(has_ties))
    def _():
        def emit(slot, carry):
            store(slot, jnp.where(keys_ref[slot] >= t, 0.0, NEG))
            return carry
        lax.fori_loop(0, nslots, emit, 0)

    @pl.when(has_ties)
    def _():
        need = n_top - count(lambda k: k > t)

        def emit(slot, seen):
            k = keys_ref[slot]
            eqf = jnp.where(k == t, 1.0, 0.0)
            rank = seen + jnp.dot(ltri_ref[...], eqf.astype(BF), preferred_element_type=F32)
            tie_ok = jnp.where(jnp.logical_and(k == t, rank < need), 0.0, NEG)
            store(slot, jnp.where(k > t, 0.0, tie_ok))
            return seen + jnp.sum(eqf, axis=0, keepdims=True)
        lax.fori_loop(0, nslots, emit, jnp.zeros((1, LANE), F32))

    def blank(j, carry):
        madd_ref[0, 0, pl.ds(pl.multiple_of(j * QT, QT), QT), :] = jnp.full((QT, LANE), NEG, F32)
        return carry

    lax.fori_loop(nslots - 1, nq, blank, 0)


def _attn_prompt_kernel(q_ref, k_ref, vt_ref, madd_ref, km_ref, vtm_ref, maddm_ref, bt_ref, o_ref,
                        m_ref, l_ref, acc_ref, *, sub, pc):
    i = pl.program_id(1)
    kb = pl.program_id(2)
    scale = ATT_DH ** -0.5

    def process(kt, vtt, madd, bias_idx):
        q = q_ref[0]
        heads = [slice(h * ATT_DH, (h + 1) * ATT_DH) for h in range(ATT_HEADS)]
        lgs = [lax.dot_general(kt[:, sl], q[:, sl], _NT, preferred_element_type=F32) for sl in heads]
        ps, alphas = [], []
        for h in range(ATT_HEADS):
            bias = jnp.concatenate([bt_ref[bi, h] for bi in bias_idx], axis=0)
            lg = lgs[h] * scale + bias + madd
            m_old = m_ref[h:h + 1, :]
            m_new = jnp.maximum(m_old, jnp.max(lg, axis=0, keepdims=True))
            alpha = jnp.exp(m_old - m_new)
            p = jnp.exp(lg - m_new)
            l_ref[h:h + 1, :] = alpha * l_ref[h:h + 1, :] + jnp.sum(p, axis=0, keepdims=True)
            m_ref[h:h + 1, :] = m_new
            ps.append(p.astype(BF))
            alphas.append(alpha)
        for h, sl in enumerate(heads):
            acc_ref[sl, :] = alphas[h] * acc_ref[sl, :] + jnp.dot(vtt[sl, :], ps[h], preferred_element_type=F32)

    @pl.when(kb == 0)
    def _():
        m_ref[...] = jnp.full(m_ref.shape, NEG, F32)
        l_ref[...] = jnp.zeros(l_ref.shape, F32)
        acc_ref[...] = jnp.zeros(acc_ref.shape, F32)
        process(km_ref[...], vtm_ref[...], maddm_ref[0, 0], [jnp.where(i == 0, 3, 2)])

    nchunk = jnp.clip((i - kb * sub + pc) // pc, 0, sub // pc)

    def body(c, carry):
        off = pl.multiple_of(c * (pc * QT), pc * QT)
        j0 = kb * sub + c * pc
        process(k_ref[0, pl.ds(off, pc * QT), :], vt_ref[0, :, pl.ds(off, pc * QT)],
                madd_ref[0, 0, pl.ds(off, pc * QT), :], [jnp.clip(i - (j0 + u), 0, 2) for u in range(pc)])
        return carry

    lax.fori_loop(0, nchunk, body, 0)

    @pl.when(kb == pl.num_programs(2) - 1)
    def _():
        for h in range(ATT_HEADS):
            sl = slice(h * ATT_DH, (h + 1) * ATT_DH)
            o_ref[0, :, sl] = (acc_ref[sl, :] * (1.0 / l_ref[h:h + 1, :])).T.astype(BF)


def _dsa_prompt(pr, me, rel_bias, b, s, n_top):
    kblk_sz = min(KBLK_MAX, s)
    assert s % kblk_sz == 0 and kblk_sz % QT == 0 and REL_MAX_DIST <= QT
    nq = s // QT
    nkb = s // kblk_sz
    sub = kblk_sz // QT
    chunk = min(KEY_CHUNK, s)
    assert kblk_sz % chunk == 0 and chunk % QT == 0
    r3 = lambda a: a.reshape(b, s, a.shape[-1])
    ikb = r3(pr["ikb"])
    iqt = pr["iq"].reshape(b, nq, QT, IDX_HEADS, LANE).transpose(0, 1, 4, 3, 2).reshape(b, nq, LANE, IDX_HEADS * QT)
    iwt = r3(pr["iw"])[..., :IDX_HEADS].transpose(0, 2, 1)
    aq, akb, avt = r3(pr["aq"]), r3(pr["akb"]), pr["avt"]
    padm = lambda a: jnp.pad(a, ((0, QT - N_META), (0, 0)))
    ikm, km, vtm = padm(me["ikb"]), padm(me["akb"]), padm(me["avb"]).T
    kk = jnp.arange(QT)
    ltri = (kk[None, :] < kk[:, None]).astype(BF)

    madd = pl.pallas_call(
        functools.partial(_select_prompt_kernel, n_top=float(n_top), seq=s, fill_kb=chunk), grid=(b, nq),
        in_specs=[pl.BlockSpec((1, 1, LANE, IDX_HEADS * QT), lambda bi, i: (bi, i, 0, 0)),
                  pl.BlockSpec((1, IDX_HEADS, QT), lambda bi, i: (bi, 0, i)),
                  pl.BlockSpec((1, s, LANE), lambda bi, i: (bi, 0, 0)),
                  pl.BlockSpec((QT, LANE), lambda bi, i: (0, 0)),
                  pl.BlockSpec((QT, QT), lambda bi, i: (0, 0))],
        out_specs=pl.BlockSpec((1, 1, s + QT, LANE), lambda bi, i: (bi, i, 0, 0)),
        out_shape=jax.ShapeDtypeStruct((b, nq, s + QT, LANE), F32),
        scratch_shapes=[pltpu.VMEM((nq + 1, QT, LANE), I32)],
        compiler_params=_cparams(("parallel", "arbitrary")), name="dsa_select_prompt",
    )(iqt, iwt, ikb, ikm, ltri)

    bb = _t5_bias_by_dist(rel_bias, 2 * QT)
    kj = jnp.arange(QT)[:, None]
    qi = jnp.arange(LANE)[None, :]
    bt = jnp.stack([bb[:, jnp.clip(qi - kj, 0, 2 * QT)],
                    bb[:, jnp.clip(QT + qi - kj, 0, 2 * QT)],
                    jnp.broadcast_to(bb[:, 2 * QT][:, None, None], (ATT_HEADS, QT, LANE)),
                    bb[:, jnp.clip(N_META + qi - kj, 0, 2 * QT)]])

    kblk = lambda i, kb: jnp.minimum(kb, i // sub)
    qmap = lambda bi, i, kb: (bi, i, 0)
    const2 = lambda bi, i, kb: (0, 0)
    return pl.pallas_call(
        functools.partial(_attn_prompt_kernel, sub=sub, pc=chunk // QT), grid=(b, nq, nkb),
        in_specs=[pl.BlockSpec((1, QT, 1024), qmap),
                  pl.BlockSpec((1, kblk_sz, 1024), lambda bi, i, kb: (bi, kblk(i, kb), 0)),
                  pl.BlockSpec((1, 1024, kblk_sz), lambda bi, i, kb: (bi, 0, kblk(i, kb))),
                  pl.BlockSpec((1, 1, kblk_sz, LANE), lambda bi, i, kb: (bi, i, kblk(i, kb), 0)),
                  pl.BlockSpec((QT, 1024), const2), pl.BlockSpec((1024, QT), const2),
                  pl.BlockSpec((1, 1, QT, LANE), lambda bi, i, kb: (bi, i, s // QT, 0)),
                  pl.BlockSpec(bt.shape, lambda bi, i, kb: (0, 0, 0, 0))],
        out_specs=pl.BlockSpec((1, QT, 1024), qmap),
        out_shape=jax.ShapeDtypeStruct((b, s, 1024), BF),
        scratch_shapes=[pltpu.VMEM((ATT_HEADS, LANE), F32), pltpu.VMEM((ATT_HEADS, LANE), F32),
                        pltpu.VMEM((1024, QT), F32)],
        compiler_params=_cparams(("parallel", "parallel", "arbitrary")), name="dsa_attn_prompt",
    )(aq, akb, avt, madd, km, vtm, madd, bt)


def _sample_scores(iq, iw, ikt):
    r = lax.dot_general(iq, ikt, _NT, preferred_element_type=F32)
    r = jnp.maximum(r, 0.0) * iw
    nq = r.shape[0] // IDX_HEADS
    s = r[0:nq]
    for h in range(1, IDX_HEADS):
        s = s + r[h * nq:(h + 1) * nq]
    return s + 0.0


def _select_sample_kernel(pt_ref, iq_ref, iw_ref, ikn_ref, utri_ref, *rest, n_top, n_pages, s_new):
    pages = rest[:PG_THR]
    madd_ref, keys_ref = rest[PG_THR], rest[PG_THR + 1]
    g = pl.program_id(1)
    iq = iq_ref[0]
    iw = iw_ref[0]
    nq = iq.shape[0] // IDX_HEADS
    nslots = n_pages + 1
    ik = jnp.concatenate([pages[u][0, 0].astype(BF) for u in range(PG_THR)], axis=0)
    s = _sample_scores(iq, iw, ik)
    for u in range(PG_THR):
        keys_ref[g * PG_THR + u] = _sort_key(s[:, u * PAGE_SIZE:(u + 1) * PAGE_SIZE])

    @pl.when(g == pl.num_programs(1) - 1)
    def _():
        col = lax.broadcasted_iota(I32, (nq, LANE), 1)
        row = lax.broadcasted_iota(I32, (nq, LANE), 0)
        k = _sort_key(_sample_scores(iq, iw, ikn_ref[0]))
        keys_ref[n_pages] = jnp.where(jnp.logical_and(col <= row, col < s_new), k, INT_MIN)
        t = _kth_largest_key(keys_ref, nslots, n_top, nq)

        def count(pred):
            def add(j, acc):
                return acc + jnp.where(pred(keys_ref[j]), 1.0, 0.0)
            return jnp.sum(lax.fori_loop(0, nslots, add, jnp.zeros((nq, LANE), F32)), axis=1, keepdims=True)

        def store(slot, madd):
            madd_ref[0, slot] = madd

        has_ties = jnp.max(count(lambda kk: kk >= t)) > n_top

        @pl.when(jnp.logical_not(has_ties))
        def _():
            def emit(slot, carry):
                store(slot, jnp.where(keys_ref[slot] >= t, 0.0, NEG))
                return carry
            lax.fori_loop(0, nslots, emit, 0)

        @pl.when(has_ties)
        def _():
            need = n_top - count(lambda kk: kk > t)

            def emit(slot, seen):
                kk = keys_ref[slot]
                eqf = jnp.where(kk == t, 1.0, 0.0)
                rank = seen + jnp.dot(eqf.astype(BF), utri_ref[...], preferred_element_type=F32)
                tie_ok = jnp.where(jnp.logical_and(kk == t, rank < need), 0.0, NEG)
                store(slot, jnp.where(kk > t, 0.0, tie_ok))
                return seen + jnp.sum(eqf, axis=1, keepdims=True)
            lax.fori_loop(0, nslots, emit, jnp.zeros((nq, 1), F32))


def _attn_sample_kernel(pt_ref, q_ref, madd_ref, maddn_ref, kn_ref, vn_ref, bts_ref, btn_ref, exp_ref, *rest):
    kp = rest[0:PG_ATT]
    vp = rest[PG_ATT:2 * PG_ATT]
    o_ref, m_ref, l_ref, acc_ref = rest[2 * PG_ATT:]
    g = pl.program_id(1)
    last = pl.num_programs(1) - 1
    scale = ATT_DH ** -0.5
    flat = PAGE_SIZE * ATT_HEADS

    @pl.when(g == 0)
    def _():
        m_ref[...] = jnp.full(m_ref.shape, NEG, F32)
        l_ref[...] = jnp.zeros(l_ref.shape, F32)
        acc_ref[...] = jnp.zeros(acc_ref.shape, F32)

    q = q_ref[0]

    def update(lg, vt):
        m_old = m_ref[...]
        m_new = jnp.maximum(m_old, jnp.max(lg, axis=1, keepdims=True))
        alpha = jnp.exp(m_old - m_new)
        p = jnp.exp(lg - m_new)
        l_ref[...] = alpha * l_ref[...] + jnp.sum(p, axis=1, keepdims=True)
        acc_ref[...] = alpha * acc_ref[...] + jnp.dot(p.astype(BF), vt, preferred_element_type=F32)
        m_ref[...] = m_new

    def spread(madd, ncols):
        sel = jnp.where(madd >= 0.0, 1.0, 0.0).astype(BF)
        rep = jnp.dot(sel, exp_ref[:, 0:ncols], preferred_element_type=F32)
        return jnp.concatenate([jnp.where(rep > 0.5, 0.0, NEG)] * ATT_HEADS, axis=0)

    ks = jnp.concatenate([kp[u][0, 0].reshape(flat, ATT_DH).astype(BF) for u in range(PG_ATT)], axis=0)
    vs = jnp.concatenate([vp[u][0, 0].reshape(flat, ATT_DH).astype(BF) for u in range(PG_ATT)], axis=0)
    madd = jnp.concatenate([spread(madd_ref[0, u], flat) for u in range(PG_ATT)], axis=1)
    bias = jnp.concatenate([bts_ref[0]] * (PG_ATT - 1) + [bts_ref[jnp.where(g == last, 1, 0)]], axis=1)
    update(lax.dot_general(q, ks, _NT, preferred_element_type=F32) * scale + bias + madd, vs)

    @pl.when(g == last)
    def _():
        lg = lax.dot_general(q, kn_ref[0], _NT, preferred_element_type=F32) * scale
        update(lg + btn_ref[...] + spread(maddn_ref[0, 0], LANE), vn_ref[0])
        o_ref[0] = acc_ref[...] / l_ref[...]


def _dsa_sample(sm, cache_k, cache_v, cache_kidx, page_table, rel_bias, db, sd):
    n_pages = page_table.shape[1]
    past = n_pages * PAGE_SIZE
    n_top = min(TOPK_MAX, (past + sd) // 4)
    assert n_pages % PG_THR == 0 and n_pages % PG_ATT == 0 and sd <= NEW_PAD and sd == 8
    rows = IDX_HEADS * sd
    iq = sm["iq"].reshape(db, sd, IDX_HEADS, LANE)[..., :IDX_DIM]
    iq = iq.transpose(0, 2, 1, 3).reshape(db, rows, IDX_DIM)
    iw = sm["iw"].reshape(db, sd, LANE)[..., :IDX_HEADS].transpose(0, 2, 1).reshape(db, rows, 1)
    ikn = jnp.pad(sm["ikb"].reshape(db, sd, LANE)[..., :IDX_DIM], ((0, 0), (0, LANE - sd), (0, 0)))
    pt = page_table.reshape(-1).astype(I32)

    def page_spec(minor, per_step, u):
        return pl.BlockSpec((1, 1, PAGE_SIZE) + minor,
                            lambda bi, g, pt_ref: (0, pt_ref[bi * n_pages + g * per_step + u]) + (0,) * (1 + len(minor)))

    kv_minor = (ATT_HEADS, ATT_DH)

    seq3 = lambda a: pl.BlockSpec((1,) + a.shape[1:], lambda bi, g, pt_ref: (bi, 0, 0))
    full = lambda a: pl.BlockSpec(a.shape, lambda bi, g, pt_ref: (0,) * a.ndim)

    kk = jnp.arange(LANE)
    utri = (kk[:, None] < kk[None, :]).astype(BF)
    madd = pl.pallas_call(
        functools.partial(_select_sample_kernel, n_top=float(n_top), n_pages=n_pages, s_new=sd),
        grid_spec=pltpu.PrefetchScalarGridSpec(
            num_scalar_prefetch=1, grid=(db, n_pages // PG_THR),
            in_specs=[seq3(iq), seq3(iw), seq3(ikn), full(utri)]
                     + [page_spec((IDX_DIM,), PG_THR, u) for u in range(PG_THR)],
            out_specs=pl.BlockSpec((1, n_pages + 1, sd, LANE), lambda bi, g, pt_ref: (bi, 0, 0, 0)),
            scratch_shapes=[pltpu.VMEM((n_pages + 1, sd, LANE), I32)]),
        out_shape=jax.ShapeDtypeStruct((db, n_pages + 1, sd, LANE), F32),
        compiler_params=_cparams(("parallel", "arbitrary")), name="dsa_select_sample",
    )(pt, iq, iw, ikn, utri, *([cache_kidx] * PG_THR))

    bb = _t5_bias_by_dist(rel_bias, 2 * PAGE_SIZE)
    qq = jnp.arange(sd)[:, None]
    tt = jnp.arange(PAGE_SIZE)[None, :]
    same_head = jnp.eye(ATT_HEADS, dtype=bool)[:, None, None, :]

    def head_lanes(tab):
        t = tab.shape[-1]
        return jnp.where(same_head, tab[..., None], NEG).reshape(ATT_HEADS * sd, t * ATT_HEADS)

    bts = jnp.stack([head_lanes(jnp.broadcast_to(bb[:, 2 * PAGE_SIZE][:, None, None], (ATT_HEADS, sd, PAGE_SIZE))),
                     head_lanes(bb[:, jnp.clip(PAGE_SIZE + qq - tt, 0, 2 * PAGE_SIZE)])])
    btn = head_lanes(bb[:, jnp.clip(qq - jnp.arange(NEW_PAD)[None, :], 0, 2 * PAGE_SIZE)])
    lane_tok = jnp.arange(PAGE_SIZE * ATT_HEADS) // ATT_HEADS
    expand = (jnp.arange(PAGE_SIZE)[:, None] == lane_tok[None, :]).astype(BF)
    q3 = sm["aq"].reshape(db, sd, ATT_HEADS, ATT_DH).transpose(0, 2, 1, 3).reshape(db, rows, ATT_DH)
    flatn = lambda a: jnp.pad(a.reshape(db, sd * ATT_HEADS, ATT_DH), ((0, 0), (0, (NEW_PAD - sd) * ATT_HEADS), (0, 0)))
    kn, vn = flatn(sm["akb"]), flatn(sm["avb"])

    out = pl.pallas_call(
        _attn_sample_kernel,
        grid_spec=pltpu.PrefetchScalarGridSpec(
            num_scalar_prefetch=1, grid=(db, n_pages // PG_ATT),
            in_specs=[seq3(q3),
                      pl.BlockSpec((1, PG_ATT, sd, LANE), lambda bi, g, pt_ref: (bi, g, 0, 0)),
                      pl.BlockSpec((1, 1, sd, LANE), lambda bi, g, pt_ref: (bi, n_pages, 0, 0)),
                      seq3(kn), seq3(vn), full(bts), full(btn), full(expand)]
                     + [page_spec(kv_minor, PG_ATT, u) for u in range(PG_ATT)]
                     + [page_spec(kv_minor, PG_ATT, u) for u in range(PG_ATT)],
            out_specs=pl.BlockSpec((1, rows, ATT_DH), lambda bi, g, pt_ref: (bi, 0, 0)),
            scratch_shapes=[pltpu.VMEM((rows, 1), F32), pltpu.VMEM((rows, 1), F32),
                            pltpu.VMEM((rows, ATT_DH), F32)]),
        out_shape=jax.ShapeDtypeStruct((db, rows, ATT_DH), F32),
        compiler_params=_cparams(("parallel", "arbitrary")), name="dsa_attn_sample",
    )(pt, q3, madd, madd, kn, vn, bts, btn, expand, *([cache_k] * PG_ATT), *([cache_v] * PG_ATT))
    return out.reshape(db, ATT_HEADS, sd, ATT_DH).transpose(0, 2, 1, 3).reshape(db, sd, ATT_HEADS * ATT_DH)


def _merge_kernel(x_ref, ro_ref, rg_ref, ao_ref, ga_ref, gb_ref, rng_ref, wr_ref, wa_ref, wo_ref, y_ref, yr_ref):
    for h in range(RET_HEADS):
        sl = slice(h * RET_DV, (h + 1) * RET_DV)
        rg = rg_ref[:, sl]
        yr_ref[:, sl] = (rg * jax.nn.sigmoid(rg) * _rms(ro_ref[:, sl], rng_ref[:, sl])).astype(BF)
    y_r = jnp.dot(yr_ref[...], wr_ref[...], preferred_element_type=F32)
    y_a = jnp.dot(ao_ref[...].astype(BF), wa_ref[...], preferred_element_type=F32)
    merged = jax.nn.sigmoid(ga_ref[...]) * y_r + jax.nn.sigmoid(gb_ref[...]) * y_a
    y_ref[...] = x_ref[...] + jnp.dot(merged.astype(BF), wo_ref[...], preferred_element_type=F32)


def _merge(x2d, ret_o, rg, att_o, ga, gb, ret_norm_g, wr, wa, wo):
    n = x2d.shape[0]
    t = min(TOK_TILE, n)
    assert n % t == 0
    row = pl.BlockSpec((t, 1024), lambda i: (i, 0))
    full = lambda a: pl.BlockSpec(a.shape, lambda i: (0,) * a.ndim)
    rng = ret_norm_g.reshape(1, RET_HEADS * RET_DV)
    return pl.pallas_call(
        _merge_kernel, grid=(n // t,),
        in_specs=[row] * 6 + [full(rng), full(wr), full(wa), full(wo)],
        out_specs=row, out_shape=jax.ShapeDtypeStruct((n, 1024), F32),
        scratch_shapes=[pltpu.VMEM((t, 1024), BF)],
        compiler_params=_cparams(("parallel",)), name="merge",
    )(x2d, ret_o, rg, att_o, ga, gb, rng, wr, wa, wo)


def _peer_route_kernel(x_ref, g_ref, wq_ref, sk1_ref, sk2_ref, hn_ref, s1_ref, s2_ref, st_ref,
                       qt_ref, v1_ref, v2_ref, c_ref, sc_ref):
    hn = _rms(x_ref[...], g_ref[...]).astype(BF)
    hn_ref[...] = hn
    qt_ref[...] = lax.dot_general(wq_ref[...], hn, _NT, preferred_element_type=F32)
    half = PEER_DQ // 2
    ninf = -jnp.inf

    def top_vals(src_ref, h, dst_ref):
        x = src_ref[h]
        for r in range(PEER_TOPK):
            v = jnp.max(x, axis=0, keepdims=True)
            dst_ref[r:r + 1, :] = v
            x = jnp.where(x == v, ninf, x)

    def head(h, carry):
        o = pl.multiple_of(h * PEER_DQ, PEER_DQ)
        q1 = qt_ref[pl.ds(o, half), :].astype(BF)
        q2 = qt_ref[pl.ds(o + half, half), :].astype(BF)
        s1_ref[h] = jnp.dot(sk1_ref[h], q1, preferred_element_type=F32)
        s2_ref[h] = jnp.dot(sk2_ref[h], q2, preferred_element_type=F32)
        top_vals(s1_ref, h, v1_ref)
        top_vals(s2_ref, h, v2_ref)
        v2 = v2_ref[...]
        for a in range(PEER_TOPK):
            c_ref[a * PEER_TOPK:(a + 1) * PEER_TOPK, :] = v1_ref[a:a + 1, :] + v2
        for r in range(PEER_TOPK):
            c = c_ref[...]
            v = jnp.max(c, axis=0, keepdims=True)
            sc_ref[r:r + 1, :] = v
            c_ref[...] = jnp.where(c == v, ninf, c)
        sc = sc_ref[...]
        z = jnp.sum(jnp.exp(sc - sc[0:1, :]), axis=0, keepdims=True)
        st_ref[h, 0:1, :] = sc[PEER_TOPK - 1:PEER_TOPK, :]
        st_ref[h, 1:2, :] = v1_ref[0:1, :]
        st_ref[h, 2:3, :] = v2_ref[0:1, :]
        st_ref[h, 3:4, :] = 1.0 / z
        st_ref[h, 4:8, :] = jnp.zeros((4, z.shape[1]), F32)
        return carry

    lax.fori_loop(0, PEER_HEADS, head, 0)


def _peer_expert_kernel(hn_ref, x_ref, s1_ref, s2_ref, st_ref, u_ref, vt_ref, y_ref,
                        e1_ref, e2_ref, act_ref, w_ref, out_ref):
    c = pl.program_id(1)
    slabs = EXP_CHUNK // PEER_KEYS

    @pl.when(c == 0)
    def _():
        for h in range(PEER_HEADS):
            st = st_ref[h]
            e1_ref[h] = jnp.exp(s1_ref[h] - st[1:2, :]) * st[3:4, :]
            e2_ref[h] = jnp.exp(s2_ref[h] - st[2:3, :])
        out_ref[...] = jnp.zeros(out_ref.shape, F32)

    act_ref[...] = lax.dot_general(u_ref[...], hn_ref[...], _NT, preferred_element_type=F32)

    def slab(al, carry):
        a = c * slabs + al
        g = None
        for h in range(PEER_HEADS):
            ssum = s1_ref[h, pl.ds(a, 1), :] + s2_ref[h]
            gh = jnp.where(ssum >= st_ref[h, 0:1, :], e2_ref[h], 0.0) * e1_ref[h, pl.ds(a, 1), :]
            g = gh if g is None else g + gh
        o = pl.multiple_of(al * PEER_KEYS, PEER_KEYS)
        x = act_ref[pl.ds(o, PEER_KEYS), :]
        gelu = 0.5 * x * (1.0 + lax.erf(x * math.sqrt(0.5)))
        w_ref[pl.ds(o, PEER_KEYS), :] = (g * gelu).astype(BF)
        return carry

    lax.fori_loop(0, slabs, slab, 0)
    out_ref[...] += jnp.dot(vt_ref[...], w_ref[...], preferred_element_type=F32)

    @pl.when(c == pl.num_programs(1) - 1)
    def _():
        y_ref[...] = x_ref[...] + out_ref[...].T


def _peer(x2d, norm_g, wq_t, sk1, sk2, u_bf, vt_bf):
    n = x2d.shape[0]
    t = min(TOK_TILE, n)
    assert n % t == 0 and t % LANE == 0
    nt = n // t
    n_exp = u_bf.shape[0]
    assert n_exp == PEER_KEYS * PEER_KEYS and n_exp % EXP_CHUNK == 0
    g = norm_g.reshape(1, D_MODEL)
    row = lambda width: pl.BlockSpec((t, width), lambda i: (i, 0))
    full = lambda a: pl.BlockSpec(a.shape, lambda i: (0,) * a.ndim)
    tk = pl.BlockSpec((PEER_HEADS, PEER_KEYS, t), lambda i: (0, 0, i))
    hn, s1, s2, st = pl.pallas_call(
        _peer_route_kernel, grid=(nt,),
        in_specs=[row(D_MODEL), full(g), full(wq_t), full(sk1), full(sk2)],
        out_specs=[row(D_MODEL), tk, tk, pl.BlockSpec((PEER_HEADS, 8, t), lambda i: (0, 0, i))],
        out_shape=[jax.ShapeDtypeStruct((n, D_MODEL), BF),
                   jax.ShapeDtypeStruct((PEER_HEADS, PEER_KEYS, n), F32),
                   jax.ShapeDtypeStruct((PEER_HEADS, PEER_KEYS, n), F32),
                   jax.ShapeDtypeStruct((PEER_HEADS, 8, n), F32)],
        scratch_shapes=[pltpu.VMEM((PEER_HEADS * PEER_DQ, t), F32),
                        pltpu.VMEM((PEER_TOPK, t), F32), pltpu.VMEM((PEER_TOPK, t), F32),
                        pltpu.VMEM((PEER_TOPK * PEER_TOPK, t), F32), pltpu.VMEM((PEER_TOPK, t), F32)],
        compiler_params=_cparams(("parallel",)), name="peer_route",
    )(x2d, g, wq_t, sk1, sk2)

    tk2 = pl.BlockSpec((PEER_HEADS, PEER_KEYS, t), lambda i, c: (0, 0, i))
    row2 = lambda width: pl.BlockSpec((t, width), lambda i, c: (i, 0))
    return pl.pallas_call(
        _peer_expert_kernel, grid=(nt, n_exp // EXP_CHUNK),
        in_specs=[row2(D_MODEL), row2(D_MODEL), tk2, tk2,
                  pl.BlockSpec((PEER_HEADS, 8, t), lambda i, c: (0, 0, i)),
                  pl.BlockSpec((EXP_CHUNK, D_MODEL), lambda i, c: (c, 0)),
                  pl.BlockSpec((D_MODEL, EXP_CHUNK), lambda i, c: (0, c))],
        out_specs=row2(D_MODEL), out_shape=jax.ShapeDtypeStruct((n, D_MODEL), F32),
        scratch_shapes=[pltpu.VMEM((PEER_HEADS, PEER_KEYS, t), F32), pltpu.VMEM((PEER_HEADS, PEER_KEYS, t), F32),
                        pltpu.VMEM((EXP_CHUNK, t), F32), pltpu.VMEM((EXP_CHUNK, t), BF),
                        pltpu.VMEM((D_MODEL, t), F32)],
        compiler_params=_cparams(("parallel", "arbitrary")), name="peer_expert",
    )(hn, x2d, s1, s2, st, u_bf, vt_bf)


def _rope_tables(pos):
    half = RET_DK // 2
    inv = ROPE_BASE ** (-jnp.arange(half, dtype=F32) / half)
    ang = pos.astype(F32)[:, None] * inv[None, :]
    cos, sin = jnp.cos(ang), jnp.sin(ang)
    return jnp.concatenate([cos, cos], axis=1), jnp.concatenate([-sin, sin], axis=1)


def _split_w_in(w):
    sizes = (RET_HEADS * RET_DK, RET_HEADS * RET_DK, RET_HEADS * RET_DV, RET_HEADS * RET_DV,
             ATT_HEADS * ATT_DH, ATT_HEADS * ATT_DH, ATT_HEADS * ATT_DH,
             IDX_HEADS * IDX_DIM, IDX_DIM, IDX_HEADS, D_MODEL, D_MODEL)
    assert w.shape[1] == sum(sizes)
    parts, o = [], 0
    for s in sizes:
        parts.append(w[:, o:o + s])
        o += s
    rq, rk, rv, rg, aq, ak, av, iq, ik, iw, ga, gb = parts
    d = w.shape[0]
    iq = jnp.pad(iq.reshape(d, IDX_HEADS, IDX_DIM), ((0, 0), (0, 0), (0, LANE - IDX_DIM))).reshape(d, -1)
    ik = jnp.pad(ik, ((0, 0), (0, LANE - IDX_DIM)))
    iw = jnp.pad(iw, ((0, 0), (0, LANE - IDX_HEADS)))
    cat = lambda xs: jnp.concatenate(xs, axis=1).astype(BF)
    return cat([rq, rk, rv, rg]), cat([aq, ak, av]), cat([iq, ik, iw, ga, gb])


def kernel(x_prompt, x_sample, cache_k, cache_v, cache_kidx, state_ret, page_table, meta_tokens, rel_bias,
           attn_norm_g, w_in, q_norm_g, k_norm_g, ret_norm_g, w_ret_o, w_att_o, w_out, ffn_norm_g,
           peer_w_q, peer_sub_keys, peer_u, peer_v):
    b, s, d = x_prompt.shape
    db, sd, _ = x_sample.shape
    assert w_in.shape[0] == 1 and d == D_MODEL and s % RET_CHUNK == 0
    n_pages = page_table.shape[1]
    past = n_pages * PAGE_SIZE
    n_top_p = min(TOPK_MAX, s // 4)
    l = 0

    wts = _split_w_in(w_in[l])
    qg = q_norm_g[l].reshape(1, ATT_DH)
    kg = k_norm_g[l].reshape(1, ATT_DH)
    wr, wa, wo = w_ret_o[l].astype(BF), w_att_o[l].astype(BF), w_out[l].astype(BF)
    wq_t = peer_w_q[l].T.astype(BF)
    sk1 = peer_sub_keys[l][:, 0].astype(BF)
    sk2 = peer_sub_keys[l][:, 1].astype(BF)
    u_bf = peer_u[l].astype(BF)
    vt_bf = peer_v[l].T.astype(BF)
    hkv = (ATT_HEADS, ATT_DH)

    xr = x_prompt.reshape(b * s, d)
    pr = _in_proj(xr, attn_norm_g[l], wts, *_rope_tables(N_META + jnp.arange(s)), qg, kg, vt_seq_len=s)
    me = _in_proj(meta_tokens.astype(x_prompt.dtype), attn_norm_g[l], wts, *_rope_tables(jnp.arange(N_META)),
                  qg, kg)
    m3 = lambda a: a.reshape(1, N_META, a.shape[-1])
    _, s_meta = _retention(m3(me["rq"]), m3(me["rk"]), m3(me["rv"]),
                           jnp.zeros((1, RET_HEADS, RET_DK, RET_DV), F32), N_META)
    r3 = lambda a: a.reshape(b, s, a.shape[-1])
    ret_o, s_fin = _retention(r3(pr["rq"]), r3(pr["rk"]), r3(pr["rv"]), s_meta, RET_CHUNK)
    att_o = _dsa_prompt(pr, me, rel_bias, b, s, n_top_p)
    x1 = _merge(xr, ret_o.reshape(b * s, -1), pr["rg"], att_o.reshape(b * s, -1), pr["ga"], pr["gb"],
                ret_norm_g[l], wr, wa, wo)
    y_prompt = _peer(x1, ffn_norm_g[l], wq_t, sk1, sk2, u_bf, vt_bf).reshape(b, s, d)

    def with_meta(real, meta):
        meta = jnp.broadcast_to(meta[None], (b,) + meta.shape)
        return jnp.concatenate([meta, real.reshape(b, s, -1)], axis=1)

    k_prompt = with_meta(pr["ak"], me["ak"]).reshape(1, b, s + N_META, *hkv).astype(cache_k.dtype)
    v_prompt = with_meta(pr["av"], me["av"]).reshape(1, b, s + N_META, *hkv).astype(cache_v.dtype)
    kidx_prompt = with_meta(pr["ik"], me["ik"])[None].astype(cache_kidx.dtype)
    ret_state_prompt = s_fin[None].astype(state_ret.dtype)

    xs = x_sample.reshape(db * sd, d)
    pos_s = jnp.tile(past + jnp.arange(sd), (db * sd) // sd)
    sm = _in_proj(xs, attn_norm_g[l], wts, *_rope_tables(pos_s), qg, kg)
    s3 = lambda a: a.reshape(db, sd, a.shape[-1])
    ret_os, s_new = _retention(s3(sm["rq"]), s3(sm["rk"]), s3(sm["rv"]), state_ret[l].astype(F32), sd)
    att_os = _dsa_sample(sm, cache_k, cache_v, cache_kidx, page_table, rel_bias, db, sd)
    x1s = _merge(xs, ret_os.reshape(db * sd, -1), sm["rg"], att_os.reshape(db * sd, -1), sm["ga"], sm["gb"],
                 ret_norm_g[l], wr, wa, wo)
    y_sample = _peer(x1s, ffn_norm_g[l], wq_t, sk1, sk2, u_bf, vt_bf).reshape(db, sd, d)

    k_sample = sm["ak"].reshape(1, db, sd, *hkv).astype(cache_k.dtype)
    v_sample = sm["av"].reshape(1, db, sd, *hkv).astype(cache_v.dtype)
    kidx_sample = sm["ik"].reshape(1, db, sd, IDX_DIM).astype(cache_kidx.dtype)
    ret_state_sample = s_new[None].astype(state_ret.dtype)

    return (y_prompt, y_sample, k_prompt, v_prompt, kidx_prompt, ret_state_prompt,
            k_sample, v_sample, kidx_sample, ret_state_sample)
```

```python
import functools
import math

import jax
import jax.numpy as jnp
from jax import lax
from jax.experimental import pallas as pl
from jax.experimental.pallas import tpu as pltpu

F32 = jnp.float32
BF = jnp.bfloat16
I32 = jnp.int32

D_MODEL = 1024
PAGE_SIZE = 128
N_META = 16
RET_HEADS = 4
RET_DK = 128
RET_DV = 256
RET_CHUNK = 128
ROPE_BASE = 10000.0
ATT_HEADS = 8
ATT_DH = 128
IDX_HEADS = 8
IDX_DIM = 64
TOPK_MAX = 256
REL_BUCKETS = 32
REL_MAX_DIST = 128
PEER_HEADS = 8
PEER_KEYS = 128
PEER_DQ = 128
PEER_TOPK = 16
CAND_ROWS = PEER_TOPK // 2
EPS = 1e-6
NEG = -1e30
LOG2E = math.log2(math.e)

LANE = 128
TOK_TILE = 512
QT = 128
KBLK_MAX = 2048
KEY_CHUNK = 512
EXP_CHUNK = 1024
PG_THR = 16
PG_ATT = 8
NEW_PAD = 16
VMEM_LIMIT = 52 * 1024 * 1024
INT_MIN = -2 ** 31
KEY_NEG_INF = -2139095041

_NT = (((1,), (1,)), ((), ()))
_TN = (((0,), (0,)), ((), ()))


def _cparams(sem):
    return pltpu.CompilerParams(dimension_semantics=sem, vmem_limit_bytes=VMEM_LIMIT)


def _rms(x, g):
    return x * lax.rsqrt(jnp.mean(x * x, axis=-1, keepdims=True) + EPS) * g


def _sort_key(s):
    b = lax.bitcast_convert_type(s, I32)
    return b ^ ((b >> 31) & 0x7FFFFFFF)


def _proj_ret_kernel(x_ref, g_ref, w_ref, cos_ref, sin_ref, rq_ref, rk_ref, rv_ref, rg_ref):
    h = _rms(x_ref[...], g_ref[...]).astype(BF)
    cosf = cos_ref[...]
    sinf = sin_ref[...]
    nqk = RET_HEADS * RET_DK
    for off, ref, sc in ((0, rq_ref, None), (nqk, rk_ref, RET_DK ** -0.5)):
        z = jnp.dot(h, w_ref[:, off:off + nqk], preferred_element_type=F32)
        for hd in range(RET_HEADS):
            sl = slice(hd * RET_DK, (hd + 1) * RET_DK)
            xh = z[:, sl]
            r = xh * cosf + pltpu.roll(xh, RET_DK // 2, 1) * sinf
            ref[:, sl] = r if sc is None else r * sc
    nv = RET_HEADS * RET_DV
    rv_ref[...] = jnp.dot(h, w_ref[:, 2 * nqk:2 * nqk + nv], preferred_element_type=F32)
    rg_ref[...] = jnp.dot(h, w_ref[:, 2 * nqk + nv:2 * nqk + 2 * nv], preferred_element_type=F32)


def _proj_att_kernel(x_ref, g_ref, w_ref, qg_ref, kg_ref, aq_ref, ak_ref, av_ref, akb_ref, avb_ref, *maybe_avt_ref):
    h = _rms(x_ref[...], g_ref[...]).astype(BF)
    n = ATT_HEADS * ATT_DH
    q = jnp.dot(h, w_ref[:, 0:n], preferred_element_type=F32)
    k = jnp.dot(h, w_ref[:, n:2 * n], preferred_element_type=F32)
    for hd in range(ATT_HEADS):
        sl = slice(hd * ATT_DH, (hd + 1) * ATT_DH)
        aq_ref[:, sl] = _rms(q[:, sl], qg_ref[...]).astype(BF)
        kn = _rms(k[:, sl], kg_ref[...])
        ak_ref[:, sl] = kn
        akb_ref[:, sl] = kn.astype(BF)
    v = jnp.dot(h, w_ref[:, 2 * n:3 * n], preferred_element_type=F32)
    av_ref[...] = v
    avb_ref[...] = v.astype(BF)
    for avt_ref in maybe_avt_ref:
        avt_ref[0] = v.T.astype(BF)


def _proj_idx_kernel(x_ref, g_ref, w_ref, iq_ref, ik_ref, ikb_ref, iw_ref, ga_ref, gb_ref):
    h = _rms(x_ref[...], g_ref[...]).astype(BF)
    nq = IDX_HEADS * LANE
    iq = jnp.dot(h, w_ref[:, 0:nq], preferred_element_type=F32)
    iq_ref[...] = (iq * IDX_DIM ** -0.5).astype(BF)
    ik = jnp.dot(h, w_ref[:, nq:nq + LANE], preferred_element_type=F32)
    ik_ref[...] = ik[:, 0:IDX_DIM]
    ikb_ref[...] = ik.astype(BF)
    iw = jnp.dot(h, w_ref[:, nq + LANE:nq + 2 * LANE], preferred_element_type=F32)
    iw_ref[...] = iw * IDX_HEADS ** -0.5
    o = nq + 2 * LANE
    ga_ref[...] = jnp.dot(h, w_ref[:, o:o + D_MODEL], preferred_element_type=F32)
    gb_ref[...] = jnp.dot(h, w_ref[:, o + D_MODEL:o + 2 * D_MODEL], preferred_element_type=F32)


def _in_proj(x2d, norm_g, wts, cosf, sinf, qg, kg, vt_seq_len=None):
    n = x2d.shape[0]
    t = min(TOK_TILE, n)
    assert n % t == 0 and cosf.shape[0] % t == 0
    nt = n // t
    ptiles = cosf.shape[0] // t
    w1, w2, w3 = wts
    row = lambda width: pl.BlockSpec((t, width), lambda i: (i, 0))
    full = lambda a: pl.BlockSpec(a.shape, lambda i: (0,) * a.ndim)
    tab = pl.BlockSpec((t, LANE), lambda i: (i % ptiles, 0))
    sds = lambda width, dt: jax.ShapeDtypeStruct((n, width), dt)
    g = norm_g.reshape(1, D_MODEL)
    cp = _cparams(("parallel",))

    rq, rk, rv, rg = pl.pallas_call(
        _proj_ret_kernel, grid=(nt,),
        in_specs=[row(D_MODEL), full(g), full(w1), tab, tab],
        out_specs=[row(512), row(512), row(1024), row(1024)],
        out_shape=[sds(512, F32), sds(512, F32), sds(1024, F32), sds(1024, F32)],
        compiler_params=cp, name="proj_ret")(x2d, g, w1, cosf, sinf)

    att_specs = [row(1024)] * 5
    att_shapes = [sds(1024, BF), sds(1024, F32), sds(1024, F32), sds(1024, BF), sds(1024, BF)]
    if vt_seq_len is not None:
        assert vt_seq_len % t == 0 and n % vt_seq_len == 0 and t % LANE == 0
        tps = vt_seq_len // t
        att_specs.append(pl.BlockSpec((1, 1024, t), lambda i: (i // tps, 0, i % tps)))
        att_shapes.append(jax.ShapeDtypeStruct((n // vt_seq_len, 1024, vt_seq_len), BF))
    att_out = pl.pallas_call(
        _proj_att_kernel, grid=(nt,),
        in_specs=[row(D_MODEL), full(g), full(w2), full(qg), full(kg)],
        out_specs=att_specs, out_shape=att_shapes,
        compiler_params=cp, name="proj_att")(x2d, g, w2, qg, kg)
    aq, ak, av, akb, avb = att_out[:5]
    avt = att_out[5] if vt_seq_len is not None else None

    iq, ik, ikb, iw, ga, gb = pl.pallas_call(
        _proj_idx_kernel, grid=(nt,),
        in_specs=[row(D_MODEL), full(g), full(w3)],
        out_specs=[row(1024), row(IDX_DIM), row(LANE), row(LANE), row(1024), row(1024)],
        out_shape=[sds(1024, BF), sds(IDX_DIM, F32), sds(LANE, BF), sds(LANE, F32),
                   sds(1024, F32), sds(1024, F32)],
        compiler_params=cp, name="proj_idx")(x2d, g, w3)

    return dict(rq=rq, rk=rk, rv=rv, rg=rg, aq=aq, ak=ak, av=av, akb=akb, avb=avb, avt=avt,
                iq=iq, ik=ik, ikb=ikb, iw=iw, ga=ga, gb=gb)


def _ret_kernel(q_ref, k_ref, v_ref, s0_ref, dmat_ref, dq_ref, dk_ref, gc_ref, o_ref, sout_ref, st_ref):
    c = pl.program_id(1)

    @pl.when(c == 0)
    def _():
        st_ref[...] = s0_ref[0]

    for h in range(RET_HEADS):
        qh = q_ref[0, :, h * RET_DK:(h + 1) * RET_DK].astype(BF)
        kh = k_ref[0, :, h * RET_DK:(h + 1) * RET_DK]
        vh = v_ref[0, :, h * RET_DV:(h + 1) * RET_DV].astype(BF)
        s_old = st_ref[h]
        sc = lax.dot_general(qh, kh.astype(BF), _NT, preferred_element_type=F32) * dmat_ref[h]
        intra = jnp.dot(sc.astype(BF), vh, preferred_element_type=F32)
        inter = jnp.dot(qh, s_old.astype(BF), preferred_element_type=F32) * dq_ref[h]
        o_ref[0, :, h * RET_DV:(h + 1) * RET_DV] = intra + inter
        kd = (kh * dk_ref[h]).astype(BF)
        upd = lax.dot_general(kd, vh, _TN, preferred_element_type=F32)
        st_ref[h] = s_old * gc_ref[h] + upd

    @pl.when(c == pl.num_programs(1) - 1)
    def _():
        sout_ref[0] = st_ref[...]


def _retention(q, k, v, s0, chunk):
    b, t, _ = q.shape
    assert t % chunk == 0
    nc = t // chunk
    ld = jnp.log1p(-jnp.exp2(-5.0 - jnp.arange(RET_HEADS, dtype=F32)))
    i = jnp.arange(chunk, dtype=F32)
    diff = i[:, None] - i[None, :]
    dmat = jnp.where(diff >= 0, jnp.exp(ld[:, None, None] * jnp.maximum(diff, 0.0)[None]), 0.0)
    dq = jnp.exp((i[None, :] + 1.0) * ld[:, None])[..., None]
    dk = jnp.exp((chunk - 1.0 - i)[None, :] * ld[:, None])[..., None]
    gc = jnp.exp(chunk * ld).reshape(RET_HEADS, 1, 1)
    shared = s0.shape[0] == 1
    full = lambda a: pl.BlockSpec(a.shape, lambda bi, ci: (0,) * a.ndim)
    seq = lambda width: pl.BlockSpec((1, chunk, width), lambda bi, ci: (bi, ci, 0))
    st_spec = pl.BlockSpec((1, RET_HEADS, RET_DK, RET_DV),
                           (lambda bi, ci: (0, 0, 0, 0)) if shared else (lambda bi, ci: (bi, 0, 0, 0)))
    return pl.pallas_call(
        _ret_kernel, grid=(b, nc),
        in_specs=[seq(RET_HEADS * RET_DK), seq(RET_HEADS * RET_DK), seq(RET_HEADS * RET_DV),
                  st_spec, full(dmat), full(dq), full(dk), full(gc)],
        out_specs=[seq(RET_HEADS * RET_DV),
                   pl.BlockSpec((1, RET_HEADS, RET_DK, RET_DV), lambda bi, ci: (bi, 0, 0, 0))],
        out_shape=[jax.ShapeDtypeStruct((b, t, RET_HEADS * RET_DV), F32),
                   jax.ShapeDtypeStruct((b, RET_HEADS, RET_DK, RET_DV), F32)],
        scratch_shapes=[pltpu.VMEM((RET_HEADS, RET_DK, RET_DV), F32)],
        compiler_params=_cparams(("parallel", "arbitrary")), name="retention",
    )(q, k, v, s0, dmat, dq, dk, gc)


def _t5_bias_by_dist(rel_bias, maxd):
    d = jnp.arange(maxd + 1, dtype=I32)
    max_exact = REL_BUCKETS // 2
    df = jnp.maximum(d, 1).astype(F32)
    large = max_exact + (jnp.log(df / max_exact) / math.log(REL_MAX_DIST / max_exact)
                         * (REL_BUCKETS - max_exact)).astype(I32)
    large = jnp.minimum(large, REL_BUCKETS - 1)
    bucket = jnp.where(d < max_exact, d, large)
    return rel_bias[bucket].T.astype(F32)


def _idx_scores_t(ikt, iqt, w):
    nq = iqt.shape[1] // IDX_HEADS
    r = jnp.dot(ikt, iqt, preferred_element_type=F32)
    s = None
    for h in range(IDX_HEADS):
        rh = jnp.maximum(r[:, h * nq:(h + 1) * nq], 0.0) * w[h:h + 1, :]
        s = rh if s is None else s + rh
    return s + 0.0


def _kth_largest_key(keys_ref, ntiles, n_top, rows):
    def step(sidx, t):
        bitv = jnp.left_shift(jnp.int32(1), 31 - sidx)
        cand = ((t ^ INT_MIN) | bitv) ^ INT_MIN
        cb = jnp.broadcast_to(cand, (rows, LANE))

        def count(j, acc):
            return acc + jnp.where(keys_ref[j] >= cb, 1.0, 0.0)

        acc = lax.fori_loop(0, ntiles, count, jnp.zeros((rows, LANE), F32))
        cnt = jnp.sum(acc, axis=1, keepdims=True)
        return jnp.where(cnt >= n_top, cand, t)

    t = lax.fori_loop(0, 32, step, jnp.full((rows, 1), INT_MIN, I32))
    return jnp.maximum(t, KEY_NEG_INF)


def _select_prompt_kernel(iq_ref, iwt_ref, ik_ref, ikm_ref, ltri_ref, madd_ref, keys_ref, *, n_top, seq, fill_kb):
    i = pl.program_id(1)
    nq = pl.num_programs(1)
    iqt = iq_ref[0, 0]
    w = iwt_ref[0]
    krow = lax.broadcasted_iota(I32, (QT, LANE), 0)
    qcol = lax.broadcasted_iota(I32, (QT, LANE), 1)
    big = 1e9

    keys_ref[0] = jnp.where(krow < N_META, _sort_key(_idx_scores_t(ikm_ref[...], iqt, w)), INT_MIN)
    per_blk = fill_kb // QT
    nblk = (i + per_blk) // per_blk

    def fill(blk, carry):
        s = _idx_scores_t(ik_ref[0, pl.ds(pl.multiple_of(blk * fill_kb, fill_kb), fill_kb), :], iqt, w)
        for u in range(per_blk):
            j = blk * per_blk + u
            visible = (krow + j * QT) <= (qcol + i * QT)
            keys_ref[j + 1] = jnp.where(visible, _sort_key(s[u * QT:(u + 1) * QT]), INT_MIN)
        return carry

    lax.fori_loop(0, nblk, fill, 0)
    nslots = nblk * per_blk + 1

    def count(pred):
        def add(blk, acc):
            for u in range(per_blk):
                acc = acc + jnp.where(pred(keys_ref[blk * per_blk + u + 1]), 1.0, 0.0)
            return acc
        acc = lax.fori_loop(0, nblk, add, jnp.where(pred(keys_ref[0]), 1.0, 0.0))
        return jnp.sum(acc, axis=0, keepdims=True)

    def cond(st):
        return jnp.logical_and(st[0] < 32, st[3] > 0)

    def step(st):
        sidx, t, cnt_t, _ = st
        bitv = jnp.left_shift(jnp.int32(1), 31 - sidx)
        cand = ((t ^ INT_MIN) | bitv) ^ INT_MIN
        cnt = count(lambda k: k >= cand)
        take = cnt >= n_top
        t = jnp.where(take, cand, t)
        cnt_t = jnp.where(take, cnt, cnt_t)
        return sidx + 1, t, cnt_t, (jnp.max(cnt_t) > n_top).astype(I32)

    _, t, cnt_t, _ = lax.while_loop(
        cond, step, (jnp.int32(0), jnp.full((1, LANE), INT_MIN, I32), jnp.full((1, LANE), big, F32), jnp.int32(1)))
    t = jnp.maximum(t, KEY_NEG_INF)
    has_ties = jnp.max(jnp.where(cnt_t < 0.5 * big, cnt_t, 0.0)) > n_top

    def store(slot, madd):
        @pl.when(slot == 0)
        def _():
            madd_ref[0, 0, seq:seq + QT, :] = madd

        @pl.when(slot > 0)
        def _():
            madd_ref[0, 0, pl.ds(pl.multiple_of((slot - 1) * QT, QT), QT), :] = madd

    @pl.when(jnp.logical_not(has_ties))
    def _():
        def emit(slot, carry):
            store(slot, jnp.where(keys_ref[slot] >= t, 0.0, NEG))
            return carry
        lax.fori_loop(0, nslots, emit, 0)

    @pl.when(has_ties)
    def _():
        need = n_top - count(lambda k: k > t)

        def emit(slot, seen):
            k = keys_ref[slot]
            eqf = jnp.where(k == t, 1.0, 0.0)
            rank = seen + jnp.dot(ltri_ref[...], eqf.astype(BF), preferred_element_type=F32)
            tie_ok = jnp.where(jnp.logical_and(k == t, rank < need), 0.0, NEG)
            store(slot, jnp.where(k > t, 0.0, tie_ok))
            return seen + jnp.sum(eqf, axis=0, keepdims=True)
        lax.fori_loop(0, nslots, emit, jnp.zeros((1, LANE), F32))

    def blank(j, carry):
        madd_ref[0, 0, pl.ds(pl.multiple_of(j * QT, QT), QT), :] = jnp.full((QT, LANE), NEG, F32)
        return carry

    lax.fori_loop(nslots - 1, nq, blank, 0)


def _attn_prompt_kernel(q_ref, k_ref, vt_ref, madd_ref, km_ref, vtm_ref, maddm_ref, bt_ref, o_ref,
                        m_ref, l_ref, acc_ref, *, sub, pc):
    i = pl.program_id(1)
    kb = pl.program_id(2)
    scale = ATT_DH ** -0.5

    def process(kt, vtt, madd, bias_idx):
        q = q_ref[0]
        heads = [slice(h * ATT_DH, (h + 1) * ATT_DH) for h in range(ATT_HEADS)]
        lgs = [lax.dot_general(kt[:, sl], q[:, sl], _NT, preferred_element_type=F32) for sl in heads]
        ps, alphas = [], []
        for h in range(ATT_HEADS):
            bias = jnp.concatenate([bt_ref[bi, h] for bi in bias_idx], axis=0)
            lg = lgs[h] * (scale * LOG2E) + bias + madd
            m_old = m_ref[h:h + 1, :]
            m_new = jnp.maximum(m_old, jnp.max(lg, axis=0, keepdims=True))
            alpha = jnp.exp2(m_old - m_new)
            p = jnp.exp2(lg - m_new)
            l_ref[h:h + 1, :] = alpha * l_ref[h:h + 1, :] + jnp.sum(p, axis=0, keepdims=True)
            m_ref[h:h + 1, :] = m_new
            ps.append(p.astype(BF))
            alphas.append(alpha)
        for h, sl in enumerate(heads):
            acc_ref[sl, :] = alphas[h] * acc_ref[sl, :] + jnp.dot(vtt[sl, :], ps[h], preferred_element_type=F32)

    @pl.when(kb == 0)
    def _():
        m_ref[...] = jnp.full(m_ref.shape, NEG, F32)
        l_ref[...] = jnp.zeros(l_ref.shape, F32)
        acc_ref[...] = jnp.zeros(acc_ref.shape, F32)
        process(km_ref[...], vtm_ref[...], maddm_ref[0, 0], [jnp.where(i == 0, 3, 2)])

    nchunk = jnp.clip((i - kb * sub + pc) // pc, 0, sub // pc)

    def body(c, carry):
        off = pl.multiple_of(c * (pc * QT), pc * QT)
        j0 = kb * sub + c * pc
        process(k_ref[0, pl.ds(off, pc * QT), :], vt_ref[0, :, pl.ds(off, pc * QT)],
                madd_ref[0, 0, pl.ds(off, pc * QT), :], [jnp.clip(i - (j0 + u), 0, 2) for u in range(pc)])
        return carry

    lax.fori_loop(0, nchunk, body, 0)

    @pl.when(kb == pl.num_programs(2) - 1)
    def _():
        for h in range(ATT_HEADS):
            sl = slice(h * ATT_DH, (h + 1) * ATT_DH)
            o_ref[0, :, sl] = (acc_ref[sl, :] * (1.0 / l_ref[h:h + 1, :])).T.astype(BF)


def _dsa_prompt(pr, me, rel_bias, b, s, n_top):
    kblk_sz = min(KBLK_MAX, s)
    assert s % kblk_sz == 0 and kblk_sz % QT == 0 and REL_MAX_DIST <= QT
    nq = s // QT
    nkb = s // kblk_sz
    sub = kblk_sz // QT
    chunk = min(KEY_CHUNK, s)
    assert kblk_sz % chunk == 0 and chunk % QT == 0
    r3 = lambda a: a.reshape(b, s, a.shape[-1])
    ikb = r3(pr["ikb"])
    iqt = pr["iq"].reshape(b, nq, QT, IDX_HEADS, LANE).transpose(0, 1, 4, 3, 2).reshape(b, nq, LANE, IDX_HEADS * QT)
    iwt = r3(pr["iw"])[..., :IDX_HEADS].transpose(0, 2, 1)
    aq, akb, avt = r3(pr["aq"]), r3(pr["akb"]), pr["avt"]
    padm = lambda a: jnp.pad(a, ((0, QT - N_META), (0, 0)))
    ikm, km, vtm = padm(me["ikb"]), padm(me["akb"]), padm(me["avb"]).T
    kk = jnp.arange(QT)
    ltri = (kk[None, :] < kk[:, None]).astype(BF)

    madd = pl.pallas_call(
        functools.partial(_select_prompt_kernel, n_top=float(n_top), seq=s, fill_kb=chunk), grid=(b, nq),
        in_specs=[pl.BlockSpec((1, 1, LANE, IDX_HEADS * QT), lambda bi, i: (bi, i, 0, 0)),
                  pl.BlockSpec((1, IDX_HEADS, QT), lambda bi, i: (bi, 0, i)),
                  pl.BlockSpec((1, s, LANE), lambda bi, i: (bi, 0, 0)),
                  pl.BlockSpec((QT, LANE), lambda bi, i: (0, 0)),
                  pl.BlockSpec((QT, QT), lambda bi, i: (0, 0))],
        out_specs=pl.BlockSpec((1, 1, s + QT, LANE), lambda bi, i: (bi, i, 0, 0)),
        out_shape=jax.ShapeDtypeStruct((b, nq, s + QT, LANE), F32),
        scratch_shapes=[pltpu.VMEM((nq + 1, QT, LANE), I32)],
        compiler_params=_cparams(("parallel", "arbitrary")), name="dsa_select_prompt",
    )(iqt, iwt, ikb, ikm, ltri)

    bb = _t5_bias_by_dist(rel_bias, 2 * QT)
    kj = jnp.arange(QT)[:, None]
    qi = jnp.arange(LANE)[None, :]
    bt = jnp.stack([bb[:, jnp.clip(qi - kj, 0, 2 * QT)],
                    bb[:, jnp.clip(QT + qi - kj, 0, 2 * QT)],
                    jnp.broadcast_to(bb[:, 2 * QT][:, None, None], (ATT_HEADS, QT, LANE)),
                    bb[:, jnp.clip(N_META + qi - kj, 0, 2 * QT)]]) * LOG2E

    kblk = lambda i, kb: jnp.minimum(kb, i // sub)
    qmap = lambda bi, i, kb: (bi, i, 0)
    const2 = lambda bi, i, kb: (0, 0)
    return pl.pallas_call(
        functools.partial(_attn_prompt_kernel, sub=sub, pc=chunk // QT), grid=(b, nq, nkb),
        in_specs=[pl.BlockSpec((1, QT, 1024), qmap),
                  pl.BlockSpec((1, kblk_sz, 1024), lambda bi, i, kb: (bi, kblk(i, kb), 0)),
                  pl.BlockSpec((1, 1024, kblk_sz), lambda bi, i, kb: (bi, 0, kblk(i, kb))),
                  pl.BlockSpec((1, 1, kblk_sz, LANE), lambda bi, i, kb: (bi, i, kblk(i, kb), 0)),
                  pl.BlockSpec((QT, 1024), const2), pl.BlockSpec((1024, QT), const2),
                  pl.BlockSpec((1, 1, QT, LANE), lambda bi, i, kb: (bi, i, s // QT, 0)),
                  pl.BlockSpec(bt.shape, lambda bi, i, kb: (0, 0, 0, 0))],
        out_specs=pl.BlockSpec((1, QT, 1024), qmap),
        out_shape=jax.ShapeDtypeStruct((b, s, 1024), BF),
        scratch_shapes=[pltpu.VMEM((ATT_HEADS, LANE), F32), pltpu.VMEM((ATT_HEADS, LANE), F32),
                        pltpu.VMEM((1024, QT), F32)],
        compiler_params=_cparams(("parallel", "parallel", "arbitrary")), name="dsa_attn_prompt",
    )(aq, akb, avt, madd, km, vtm, madd, bt)


def _sample_scores(iq, iw, ikt):
    r = lax.dot_general(iq, ikt, _NT, preferred_element_type=F32)
    r = jnp.maximum(r, 0.0) * iw
    nq = r.shape[0] // IDX_HEADS
    s = r[0:nq]
    for h in range(1, IDX_HEADS):
        s = s + r[h * nq:(h + 1) * nq]
    return s + 0.0


def _select_sample_kernel(pt_ref, iq_ref, iw_ref, ikn_ref, utri_ref, *rest, n_top, n_pages, s_new):
    pages = rest[:PG_THR]
    madd_ref, keys_ref = rest[PG_THR], rest[PG_THR + 1]
    g = pl.program_id(1)
    iq = iq_ref[0]
    iw = iw_ref[0]
    nq = iq.shape[0] // IDX_HEADS
    nslots = n_pages + 1
    ik = jnp.concatenate([pages[u][0, 0].astype(BF) for u in range(PG_THR)], axis=0)
    s = _sample_scores(iq, iw, ik)
    for u in range(PG_THR):
        keys_ref[g * PG_THR + u] = _sort_key(s[:, u * PAGE_SIZE:(u + 1) * PAGE_SIZE])

    @pl.when(g == pl.num_programs(1) - 1)
    def _():
        col = lax.broadcasted_iota(I32, (nq, LANE), 1)
        row = lax.broadcasted_iota(I32, (nq, LANE), 0)
        k = _sort_key(_sample_scores(iq, iw, ikn_ref[0]))
        keys_ref[n_pages] = jnp.where(jnp.logical_and(col <= row, col < s_new), k, INT_MIN)
        t = _kth_largest_key(keys_ref, nslots, n_top, nq)

        def count(pred):
            def add(j, acc):
                return acc + jnp.where(pred(keys_ref[j]), 1.0, 0.0)
            return jnp.sum(lax.fori_loop(0, nslots, add, jnp.zeros((nq, LANE), F32)), axis=1, keepdims=True)

        def store(slot, madd):
            madd_ref[0, slot] = madd

        has_ties = jnp.max(count(lambda kk: kk >= t)) > n_top

        @pl.when(jnp.logical_not(has_ties))
        def _():
            def emit(slot, carry):
                store(slot, jnp.where(keys_ref[slot] >= t, 0.0, NEG))
                return carry
            lax.fori_loop(0, nslots, emit, 0)

        @pl.when(has_ties)
        def _():
            need = n_top - count(lambda kk: kk > t)

            def emit(slot, seen):
                kk = keys_ref[slot]
                eqf = jnp.where(kk == t, 1.0, 0.0)
                rank = seen + jnp.dot(eqf.astype(BF), utri_ref[...], preferred_element_type=F32)
                tie_ok = jnp.where(jnp.logical_and(kk == t, rank < need), 0.0, NEG)
                store(slot, jnp.where(kk > t, 0.0, tie_ok))
                return seen + jnp.sum(eqf, axis=1, keepdims=True)
            lax.fori_loop(0, nslots, emit, jnp.zeros((nq, 1), F32))


def _attn_sample_kernel(pt_ref, q_ref, madd_ref, maddn_ref, kn_ref, vn_ref, bts_ref, btn_ref, exp_ref, *rest):
    kp = rest[0:PG_ATT]
    vp = rest[PG_ATT:2 * PG_ATT]
    o_ref, m_ref, l_ref, acc_ref = rest[2 * PG_ATT:]
    g = pl.program_id(1)
    last = pl.num_programs(1) - 1
    scale = ATT_DH ** -0.5
    flat = PAGE_SIZE * ATT_HEADS

    @pl.when(g == 0)
    def _():
        m_ref[...] = jnp.full(m_ref.shape, NEG, F32)
        l_ref[...] = jnp.zeros(l_ref.shape, F32)
        acc_ref[...] = jnp.zeros(acc_ref.shape, F32)

    q = q_ref[0]

    def update(lg, vt):
        m_old = m_ref[...]
        m_new = jnp.maximum(m_old, jnp.max(lg, axis=1, keepdims=True))
        alpha = jnp.exp(m_old - m_new)
        p = jnp.exp(lg - m_new)
        l_ref[...] = alpha * l_ref[...] + jnp.sum(p, axis=1, keepdims=True)
        acc_ref[...] = alpha * acc_ref[...] + jnp.dot(p.astype(BF), vt, preferred_element_type=F32)
        m_ref[...] = m_new

    def spread(madd, ncols):
        sel = jnp.where(madd >= 0.0, 1.0, 0.0).astype(BF)
        rep = jnp.dot(sel, exp_ref[:, 0:ncols], preferred_element_type=F32)
        return jnp.concatenate([jnp.where(rep > 0.5, 0.0, NEG)] * ATT_HEADS, axis=0)

    ks = jnp.concatenate([kp[u][0, 0].reshape(flat, ATT_DH).astype(BF) for u in range(PG_ATT)], axis=0)
    vs = jnp.concatenate([vp[u][0, 0].reshape(flat, ATT_DH).astype(BF) for u in range(PG_ATT)], axis=0)
    madd = jnp.concatenate([spread(madd_ref[0, u], flat) for u in range(PG_ATT)], axis=1)
    bias = jnp.concatenate([bts_ref[0]] * (PG_ATT - 1) + [bts_ref[jnp.where(g == last, 1, 0)]], axis=1)
    update(lax.dot_general(q, ks, _NT, preferred_element_type=F32) * scale + bias + madd, vs)

    @pl.when(g == last)
    def _():
        lg = lax.dot_general(q, kn_ref[0], _NT, preferred_element_type=F32) * scale
        update(lg + btn_ref[...] + spread(maddn_ref[0, 0], LANE), vn_ref[0])
        o_ref[0] = acc_ref[...] / l_ref[...]


def _dsa_sample(sm, cache_k, cache_v, cache_kidx, page_table, rel_bias, db, sd):
    n_pages = page_table.shape[1]
    past = n_pages * PAGE_SIZE
    n_top = min(TOPK_MAX, (past + sd) // 4)
    assert n_pages % PG_THR == 0 and n_pages % PG_ATT == 0 and sd <= NEW_PAD and sd == 8
    rows = IDX_HEADS * sd
    iq = sm["iq"].reshape(db, sd, IDX_HEADS, LANE)[..., :IDX_DIM]
    iq = iq.transpose(0, 2, 1, 3).reshape(db, rows, IDX_DIM)
    iw = sm["iw"].reshape(db, sd, LANE)[..., :IDX_HEADS].transpose(0, 2, 1).reshape(db, rows, 1)
    ikn = jnp.pad(sm["ikb"].reshape(db, sd, LANE)[..., :IDX_DIM], ((0, 0), (0, LANE - sd), (0, 0)))
    pt = page_table.reshape(-1).astype(I32)

    def page_spec(minor, per_step, u):
        return pl.BlockSpec((1, 1, PAGE_SIZE) + minor,
                            lambda bi, g, pt_ref: (0, pt_ref[bi * n_pages + g * per_step + u]) + (0,) * (1 + len(minor)))

    kv_minor = (ATT_HEADS, ATT_DH)

    seq3 = lambda a: pl.BlockSpec((1,) + a.shape[1:], lambda bi, g, pt_ref: (bi, 0, 0))
    full = lambda a: pl.BlockSpec(a.shape, lambda bi, g, pt_ref: (0,) * a.ndim)

    kk = jnp.arange(LANE)
    utri = (kk[:, None] < kk[None, :]).astype(BF)
    madd = pl.pallas_call(
        functools.partial(_select_sample_kernel, n_top=float(n_top), n_pages=n_pages, s_new=sd),
        grid_spec=pltpu.PrefetchScalarGridSpec(
            num_scalar_prefetch=1, grid=(db, n_pages // PG_THR),
            in_specs=[seq3(iq), seq3(iw), seq3(ikn), full(utri)]
                     + [page_spec((IDX_DIM,), PG_THR, u) for u in range(PG_THR)],
            out_specs=pl.BlockSpec((1, n_pages + 1, sd, LANE), lambda bi, g, pt_ref: (bi, 0, 0, 0)),
            scratch_shapes=[pltpu.VMEM((n_pages + 1, sd, LANE), I32)]),
        out_shape=jax.ShapeDtypeStruct((db, n_pages + 1, sd, LANE), F32),
        compiler_params=_cparams(("parallel", "arbitrary")), name="dsa_select_sample",
    )(pt, iq, iw, ikn, utri, *([cache_kidx] * PG_THR))

    bb = _t5_bias_by_dist(rel_bias, 2 * PAGE_SIZE)
    qq = jnp.arange(sd)[:, None]
    tt = jnp.arange(PAGE_SIZE)[None, :]
    same_head = jnp.eye(ATT_HEADS, dtype=bool)[:, None, None, :]

    def head_lanes(tab):
        t = tab.shape[-1]
        return jnp.where(same_head, tab[..., None], NEG).reshape(ATT_HEADS * sd, t * ATT_HEADS)

    bts = jnp.stack([head_lanes(jnp.broadcast_to(bb[:, 2 * PAGE_SIZE][:, None, None], (ATT_HEADS, sd, PAGE_SIZE))),
                     head_lanes(bb[:, jnp.clip(PAGE_SIZE + qq - tt, 0, 2 * PAGE_SIZE)])])
    btn = head_lanes(bb[:, jnp.clip(qq - jnp.arange(NEW_PAD)[None, :], 0, 2 * PAGE_SIZE)])
    lane_tok = jnp.arange(PAGE_SIZE * ATT_HEADS) // ATT_HEADS
    expand = (jnp.arange(PAGE_SIZE)[:, None] == lane_tok[None, :]).astype(BF)
    q3 = sm["aq"].reshape(db, sd, ATT_HEADS, ATT_DH).transpose(0, 2, 1, 3).reshape(db, rows, ATT_DH)
    flatn = lambda a: jnp.pad(a.reshape(db, sd * ATT_HEADS, ATT_DH), ((0, 0), (0, (NEW_PAD - sd) * ATT_HEADS), (0, 0)))
    kn, vn = flatn(sm["akb"]), flatn(sm["avb"])

    out = pl.pallas_call(
        _attn_sample_kernel,
        grid_spec=pltpu.PrefetchScalarGridSpec(
            num_scalar_prefetch=1, grid=(db, n_pages // PG_ATT),
            in_specs=[seq3(q3),
                      pl.BlockSpec((1, PG_ATT, sd, LANE), lambda bi, g, pt_ref: (bi, g, 0, 0)),
                      pl.BlockSpec((1, 1, sd, LANE), lambda bi, g, pt_ref: (bi, n_pages, 0, 0)),
                      seq3(kn), seq3(vn), full(bts), full(btn), full(expand)]
                     + [page_spec(kv_minor, PG_ATT, u) for u in range(PG_ATT)]
                     + [page_spec(kv_minor, PG_ATT, u) for u in range(PG_ATT)],
            out_specs=pl.BlockSpec((1, rows, ATT_DH), lambda bi, g, pt_ref: (bi, 0, 0)),
            scratch_shapes=[pltpu.VMEM((rows, 1), F32), pltpu.VMEM((rows, 1), F32),
                            pltpu.VMEM((rows, ATT_DH), F32)]),
        out_shape=jax.ShapeDtypeStruct((db, rows, ATT_DH), F32),
        compiler_params=_cparams(("parallel", "arbitrary")), name="dsa_attn_sample",
    )(pt, q3, madd, madd, kn, vn, bts, btn, expand, *([cache_k] * PG_ATT), *([cache_v] * PG_ATT))
    return out.reshape(db, ATT_HEADS, sd, ATT_DH).transpose(0, 2, 1, 3).reshape(db, sd, ATT_HEADS * ATT_DH)


def _merge_kernel(x_ref, ro_ref, rg_ref, ao_ref, ga_ref, gb_ref, rng_ref, wr_ref, wa_ref, wo_ref, y_ref, yr_ref):
    for h in range(RET_HEADS):
        sl = slice(h * RET_DV, (h + 1) * RET_DV)
        rg = rg_ref[:, sl]
        yr_ref[:, sl] = (rg * jax.nn.sigmoid(rg) * _rms(ro_ref[:, sl], rng_ref[:, sl])).astype(BF)
    y_r = jnp.dot(yr_ref[...], wr_ref[...], preferred_element_type=F32)
    y_a = jnp.dot(ao_ref[...].astype(BF), wa_ref[...], preferred_element_type=F32)
    merged = jax.nn.sigmoid(ga_ref[...]) * y_r + jax.nn.sigmoid(gb_ref[...]) * y_a
    y_ref[...] = x_ref[...] + jnp.dot(merged.astype(BF), wo_ref[...], preferred_element_type=F32)


def _merge(x2d, ret_o, rg, att_o, ga, gb, ret_norm_g, wr, wa, wo):
    n = x2d.shape[0]
    t = min(TOK_TILE, n)
    assert n % t == 0
    row = pl.BlockSpec((t, 1024), lambda i: (i, 0))
    full = lambda a: pl.BlockSpec(a.shape, lambda i: (0,) * a.ndim)
    rng = ret_norm_g.reshape(1, RET_HEADS * RET_DV)
    return pl.pallas_call(
        _merge_kernel, grid=(n // t,),
        in_specs=[row] * 6 + [full(rng), full(wr), full(wa), full(wo)],
        out_specs=row, out_shape=jax.ShapeDtypeStruct((n, 1024), F32),
        scratch_shapes=[pltpu.VMEM((t, 1024), BF)],
        compiler_params=_cparams(("parallel",)), name="merge",
    )(x2d, ret_o, rg, att_o, ga, gb, rng, wr, wa, wo)


def _peer_route_kernel(x_ref, g_ref, wq_ref, sk1_ref, sk2_ref, hn_ref, s1_ref, s2_ref, st_ref,
                       qt_ref, v1_ref, v2_ref, c_ref, sc_ref):
    hn = _rms(x_ref[...], g_ref[...]).astype(BF)
    hn_ref[...] = hn
    qt_ref[...] = lax.dot_general(wq_ref[...], hn, _NT, preferred_element_type=F32)
    half = PEER_DQ // 2
    ninf = -jnp.inf

    def top_vals(src_ref, h, dst_ref):
        x = src_ref[h]
        for r in range(PEER_TOPK):
            v = jnp.max(x, axis=0, keepdims=True)
            dst_ref[r:r + 1, :] = v
            x = jnp.where(x == v, ninf, x)

    def head(h, carry):
        o = pl.multiple_of(h * PEER_DQ, PEER_DQ)
        q1 = qt_ref[pl.ds(o, half), :].astype(BF)
        q2 = qt_ref[pl.ds(o + half, half), :].astype(BF)
        s1_ref[h] = jnp.dot(sk1_ref[h], q1, preferred_element_type=F32)
        s2_ref[h] = jnp.dot(sk2_ref[h], q2, preferred_element_type=F32)
        top_vals(s1_ref, h, v1_ref)
        top_vals(s2_ref, h, v2_ref)
        c_ref[0:PEER_TOPK, :] = v1_ref[0:1, :] + v2_ref[...]
        v2h = v2_ref[0:CAND_ROWS, :]
        for a in range(1, PEER_TOPK):
            r0 = PEER_TOPK + (a - 1) * CAND_ROWS
            c_ref[r0:r0 + CAND_ROWS, :] = v1_ref[a:a + 1, :] + v2h
        for r in range(PEER_TOPK):
            c = c_ref[...]
            v = jnp.max(c, axis=0, keepdims=True)
            sc_ref[r:r + 1, :] = v
            c_ref[...] = jnp.where(c == v, ninf, c)
        sc = sc_ref[...]
        z = jnp.sum(jnp.exp(sc - sc[0:1, :]), axis=0, keepdims=True)
        st_ref[h, 0:1, :] = sc[PEER_TOPK - 1:PEER_TOPK, :]
        st_ref[h, 1:2, :] = v1_ref[0:1, :]
        st_ref[h, 2:3, :] = v2_ref[0:1, :]
        st_ref[h, 3:4, :] = 1.0 / z
        st_ref[h, 4:8, :] = jnp.zeros((4, z.shape[1]), F32)
        return carry

    lax.fori_loop(0, PEER_HEADS, head, 0)


def _peer_expert_kernel(hn_ref, x_ref, s1_ref, s2_ref, st_ref, u_ref, vt_ref, y_ref,
                        e1_ref, e2_ref, act_ref, w_ref, out_ref):
    c = pl.program_id(1)
    slabs = EXP_CHUNK // PEER_KEYS

    @pl.when(c == 0)
    def _():
        for h in range(PEER_HEADS):
            st = st_ref[h]
            e1_ref[h] = jnp.exp(s1_ref[h] - st[1:2, :]) * st[3:4, :]
            e2_ref[h] = jnp.exp(s2_ref[h] - st[2:3, :])
        out_ref[...] = jnp.zeros(out_ref.shape, F32)

    act_ref[...] = lax.dot_general(u_ref[...], hn_ref[...], _NT, preferred_element_type=F32)

    def slab(al, carry):
        a = c * slabs + al
        g = None
        for h in range(PEER_HEADS):
            ssum = s1_ref[h, pl.ds(a, 1), :] + s2_ref[h]
            gh = jnp.where(ssum >= st_ref[h, 0:1, :], e2_ref[h], 0.0) * e1_ref[h, pl.ds(a, 1), :]
            g = gh if g is None else g + gh
        o = pl.multiple_of(al * PEER_KEYS, PEER_KEYS)
        x = act_ref[pl.ds(o, PEER_KEYS), :]
        gelu = 0.5 * x * (1.0 + lax.erf(x * math.sqrt(0.5)))
        w_ref[pl.ds(o, PEER_KEYS), :] = (g * gelu).astype(BF)
        return carry

    lax.fori_loop(0, slabs, slab, 0)
    out_ref[...] += jnp.dot(vt_ref[...], w_ref[...], preferred_element_type=F32)

    @pl.when(c == pl.num_programs(1) - 1)
    def _():
        y_ref[...] = x_ref[...] + out_ref[...].T


def _peer(x2d, norm_g, wq_t, sk1, sk2, u_bf, vt_bf):
    n = x2d.shape[0]
    t = min(TOK_TILE, n)
    assert n % t == 0 and t % LANE == 0
    nt = n // t
    n_exp = u_bf.shape[0]
    assert n_exp == PEER_KEYS * PEER_KEYS and n_exp % EXP_CHUNK == 0
    g = norm_g.reshape(1, D_MODEL)
    row = lambda width: pl.BlockSpec((t, width), lambda i: (i, 0))
    full = lambda a: pl.BlockSpec(a.shape, lambda i: (0,) * a.ndim)
    tk = pl.BlockSpec((PEER_HEADS, PEER_KEYS, t), lambda i: (0, 0, i))
    hn, s1, s2, st = pl.pallas_call(
        _peer_route_kernel, grid=(nt,),
        in_specs=[row(D_MODEL), full(g), full(wq_t), full(sk1), full(sk2)],
        out_specs=[row(D_MODEL), tk, tk, pl.BlockSpec((PEER_HEADS, 8, t), lambda i: (0, 0, i))],
        out_shape=[jax.ShapeDtypeStruct((n, D_MODEL), BF),
                   jax.ShapeDtypeStruct((PEER_HEADS, PEER_KEYS, n), F32),
                   jax.ShapeDtypeStruct((PEER_HEADS, PEER_KEYS, n), F32),
                   jax.ShapeDtypeStruct((PEER_HEADS, 8, n), F32)],
        scratch_shapes=[pltpu.VMEM((PEER_HEADS * PEER_DQ, t), F32),
                        pltpu.VMEM((PEER_TOPK, t), F32), pltpu.VMEM((PEER_TOPK, t), F32),
                        pltpu.VMEM((PEER_TOPK + (PEER_TOPK - 1) * CAND_ROWS, t), F32),
                        pltpu.VMEM((PEER_TOPK, t), F32)],
        compiler_params=_cparams(("parallel",)), name="peer_route",
    )(x2d, g, wq_t, sk1, sk2)

    tk2 = pl.BlockSpec((PEER_HEADS, PEER_KEYS, t), lambda i, c: (0, 0, i))
    row2 = lambda width: pl.BlockSpec((t, width), lambda i, c: (i, 0))
    return pl.pallas_call(
        _peer_expert_kernel, grid=(nt, n_exp // EXP_CHUNK),
        in_specs=[row2(D_MODEL), row2(D_MODEL), tk2, tk2,
                  pl.BlockSpec((PEER_HEADS, 8, t), lambda i, c: (0, 0, i)),
                  pl.BlockSpec((EXP_CHUNK, D_MODEL), lambda i, c: (c, 0)),
                  pl.BlockSpec((D_MODEL, EXP_CHUNK), lambda i, c: (0, c))],
        out_specs=row2(D_MODEL), out_shape=jax.ShapeDtypeStruct((n, D_MODEL), F32),
        scratch_shapes=[pltpu.VMEM((PEER_HEADS, PEER_KEYS, t), F32), pltpu.VMEM((PEER_HEADS, PEER_KEYS, t), F32),
                        pltpu.VMEM((EXP_CHUNK, t), F32), pltpu.VMEM((EXP_CHUNK, t), BF),
                        pltpu.VMEM((D_MODEL, t), F32)],
        compiler_params=_cparams(("parallel", "arbitrary")), name="peer_expert",
    )(hn, x2d, s1, s2, st, u_bf, vt_bf)


def _rope_tables(pos):
    half = RET_DK // 2
    inv = ROPE_BASE ** (-jnp.arange(half, dtype=F32) / half)
    ang = pos.astype(F32)[:, None] * inv[None, :]
    cos, sin = jnp.cos(ang), jnp.sin(ang)
    return jnp.concatenate([cos, cos], axis=1), jnp.concatenate([-sin, sin], axis=1)


def _split_w_in(w):
    sizes = (RET_HEADS * RET_DK, RET_HEADS * RET_DK, RET_HEADS * RET_DV, RET_HEADS * RET_DV,
             ATT_HEADS * ATT_DH, ATT_HEADS * ATT_DH, ATT_HEADS * ATT_DH,
             IDX_HEADS * IDX_DIM, IDX_DIM, IDX_HEADS, D_MODEL, D_MODEL)
    assert w.shape[1] == sum(sizes)
    parts, o = [], 0
    for s in sizes:
        parts.append(w[:, o:o + s])
        o += s
    rq, rk, rv, rg, aq, ak, av, iq, ik, iw, ga, gb = parts
    d = w.shape[0]
    iq = jnp.pad(iq.reshape(d, IDX_HEADS, IDX_DIM), ((0, 0), (0, 0), (0, LANE - IDX_DIM))).reshape(d, -1)
    ik = jnp.pad(ik, ((0, 0), (0, LANE - IDX_DIM)))
    iw = jnp.pad(iw, ((0, 0), (0, LANE - IDX_HEADS)))
    cat = lambda xs: jnp.concatenate(xs, axis=1).astype(BF)
    return cat([rq, rk, rv, rg]), cat([aq, ak, av]), cat([iq, ik, iw, ga, gb])


def kernel(x_prompt, x_sample, cache_k, cache_v, cache_kidx, state_ret, page_table, meta_tokens, rel_bias,
           attn_norm_g, w_in, q_norm_g, k_norm_g, ret_norm_g, w_ret_o, w_att_o, w_out, ffn_norm_g,
           peer_w_q, peer_sub_keys, peer_u, peer_v):
    b, s, d = x_prompt.shape
    db, sd, _ = x_sample.shape
    assert w_in.shape[0] == 1 and d == D_MODEL and s % RET_CHUNK == 0
    n_pages = page_table.shape[1]
    past = n_pages * PAGE_SIZE
    n_top_p = min(TOPK_MAX, s // 4)
    l = 0

    wts = _split_w_in(w_in[l])
    qg = q_norm_g[l].reshape(1, ATT_DH)
    kg = k_norm_g[l].reshape(1, ATT_DH)
    wr, wa, wo = w_ret_o[l].astype(BF), w_att_o[l].astype(BF), w_out[l].astype(BF)
    wq_t = peer_w_q[l].T.astype(BF)
    sk1 = peer_sub_keys[l][:, 0].astype(BF)
    sk2 = peer_sub_keys[l][:, 1].astype(BF)
    u_bf = peer_u[l].astype(BF)
    vt_bf = peer_v[l].T.astype(BF)
    hkv = (ATT_HEADS, ATT_DH)

    xr = x_prompt.reshape(b * s, d)
    pr = _in_proj(xr, attn_norm_g[l], wts, *_rope_tables(N_META + jnp.arange(s)), qg, kg, vt_seq_len=s)
    me = _in_proj(meta_tokens.astype(x_prompt.dtype), attn_norm_g[l], wts, *_rope_tables(jnp.arange(N_META)),
                  qg, kg)
    m3 = lambda a: a.reshape(1, N_META, a.shape[-1])
    _, s_meta = _retention(m3(me["rq"]), m3(me["rk"]), m3(me["rv"]),
                           jnp.zeros((1, RET_HEADS, RET_DK, RET_DV), F32), N_META)
    r3 = lambda a: a.reshape(b, s, a.shape[-1])
    ret_o, s_fin = _retention(r3(pr["rq"]), r3(pr["rk"]), r3(pr["rv"]), s_meta, RET_CHUNK)
    att_o = _dsa_prompt(pr, me, rel_bias, b, s, n_top_p)
    x1 = _merge(xr, ret_o.reshape(b * s, -1), pr["rg"], att_o.reshape(b * s, -1), pr["ga"], pr["gb"],
                ret_norm_g[l], wr, wa, wo)
    y_prompt = _peer(x1, ffn_norm_g[l], wq_t, sk1, sk2, u_bf, vt_bf).reshape(b, s, d)

    def with_meta(real, meta):
        meta = jnp.broadcast_to(meta[None], (b,) + meta.shape)
        return jnp.concatenate([meta, real.reshape(b, s, -1)], axis=1)

    k_prompt = with_meta(pr["ak"], me["ak"]).reshape(1, b, s + N_META, *hkv).astype(cache_k.dtype)
    v_prompt = with_meta(pr["av"], me["av"]).reshape(1, b, s + N_META, *hkv).astype(cache_v.dtype)
    kidx_prompt = with_meta(pr["ik"], me["ik"])[None].astype(cache_kidx.dtype)
    ret_state_prompt = s_fin[None].astype(state_ret.dtype)

    xs = x_sample.reshape(db * sd, d)
    pos_s = jnp.tile(past + jnp.arange(sd), (db * sd) // sd)
    sm = _in_proj(xs, attn_norm_g[l], wts, *_rope_tables(pos_s), qg, kg)
    s3 = lambda a: a.reshape(db, sd, a.shape[-1])
    ret_os, s_new = _retention(s3(sm["rq"]), s3(sm["rk"]), s3(sm["rv"]), state_ret[l].astype(F32), sd)
    att_os = _dsa_sample(sm, cache_k, cache_v, cache_kidx, page_table, rel_bias, db, sd)
    x1s = _merge(xs, ret_os.reshape(db * sd, -1), sm["rg"], att_os.reshape(db * sd, -1), sm["ga"], sm["gb"],
                 ret_norm_g[l], wr, wa, wo)
    y_sample = _peer(x1s, ffn_norm_g[l], wq_t, sk1, sk2, u_bf, vt_bf).reshape(db, sd, d)

    k_sample = sm["ak"].reshape(1, db, sd, *hkv).astype(cache_k.dtype)
    v_sample = sm["av"].reshape(1, db, sd, *hkv).astype(cache_v.dtype)
    kidx_sample = sm["ik"].reshape(1, db, sd, IDX_DIM).astype(cache_kidx.dtype)
    ret_state_sample = s_new[None].astype(state_ret.dtype)

    return (y_prompt, y_sample, k_prompt, v_prompt, kidx_prompt, ret_state_prompt,
            k_sample, v_sample, kidx_sample, ret_state_sample)
```
